```python
import jax, jax.numpy as jnp
from jax import lax
import numpy as np

D_MODEL = 1024
BATCH = 4
SEQ = 4096
DEPTH = 4

CONV_DIM = 512
CONV_WIDTH = 31
SB_HEADS = 8
SB_HEAD_DIM = 64
SB_DIM = SB_HEADS * SB_HEAD_DIM
SB_BLOCK = 128
GLA_HEADS = 4
GLA_DK = 128
GLA_DV = 128
GLA_KDIM = GLA_HEADS * GLA_DK
GLA_VDIM = GLA_HEADS * GLA_DV
GLA_GATE_RANK = 16
GLA_GATE_TAU = 16.0
GLA_CHUNK = 64
N_BRANCH = 3
IN_SIZES = (2 * CONV_DIM, 3 * SB_DIM, GLA_KDIM, GLA_KDIM, GLA_VDIM, GLA_VDIM, GLA_GATE_RANK, N_BRANCH * D_MODEL)
IN_COLS = 2 * CONV_DIM + 3 * SB_DIM + 2 * GLA_KDIM + 2 * GLA_VDIM + GLA_GATE_RANK + N_BRANCH * D_MODEL
D_FF = 2816
N_EXPERTS = 8
TOP_K = 2
D_FF_EXPERT = 3584
N_ADA = 6
EPS = 1e-6

kernel_name = 'hybrid_gated_conv_stickbreak_gla_moe'


def rms_norm(x, g):
    xf = x.astype(jnp.float32)
    y = xf * lax.rsqrt(jnp.mean(xf * xf, axis=-1, keepdims=True) + EPS)
    return (y * g.astype(jnp.float32)).astype(x.dtype)


def layer_norm(x, g, b):
    xf = x.astype(jnp.float32)
    mu = jnp.mean(xf, axis=-1, keepdims=True)
    var = jnp.mean(jnp.square(xf - mu), axis=-1, keepdims=True)
    y = (xf - mu) * lax.rsqrt(var + EPS)
    return (y * g.astype(jnp.float32) + b.astype(jnp.float32)).astype(x.dtype)


def heads(t, n):
    return t.reshape(t.shape[0], t.shape[1], n, -1).transpose(0, 2, 1, 3)


def conformer_conv(u, w_dw, b_dw, ln_g, ln_b, w_out):
    a, g = jnp.split(u, 2, axis=-1)
    y = a * jax.nn.sigmoid(g)
    y = lax.conv_general_dilated(
        y, w_dw[:, None, :].astype(y.dtype), window_strides=(1,),
        padding=((CONV_WIDTH - 1, 0),),
        dimension_numbers=('NWC', 'WIO', 'NWC'),
        feature_group_count=CONV_DIM) + b_dw
    y = jax.nn.silu(layer_norm(y, ln_g, ln_b))
    return y @ w_out


def stick_breaking_attention(q, k, v):
    scale = SB_HEAD_DIM ** -0.5
    n_blocks = q.shape[2] // SB_BLOCK
    outs = []
    for i in range(n_blocks):
        t0 = i * SB_BLOCK
        t1 = t0 + SB_BLOCK
        qb = q[:, :, t0:t1]
        kb = k[:, :, :t1]
        vb = v[:, :, :t1]
        z = jnp.einsum('bhtd,bhsd->bhts', qb, kb).astype(jnp.float32) * scale
        t_idx = t0 + jnp.arange(SB_BLOCK)[:, None]
        s_idx = jnp.arange(t1)[None, :]
        causal = s_idx < t_idx
        log_beta = jax.nn.log_sigmoid(z)
        log_keep = jnp.where(causal, jax.nn.log_sigmoid(-z), 0.0)
        log_rest = lax.cumsum(log_keep, axis=3, reverse=True) - log_keep
        att = jnp.where(causal, jnp.exp(log_beta + log_rest), 0.0)
        outs.append(jnp.einsum('bhts,bhsd->bhtd', att.astype(v.dtype), vb))
    return jnp.concatenate(outs, axis=2)


def gla_chunked(q, k, v, log_a):
    B, H, S, dk = q.shape
    dv = v.shape[-1]
    n = S // GLA_CHUNK
    q = q.reshape(B, H, n, GLA_CHUNK, dk)
    k = k.reshape(B, H, n, GLA_CHUNK, dk)
    v = v.reshape(B, H, n, GLA_CHUNK, dv)
    log_a = log_a.reshape(B, H, n, GLA_CHUNK, dk)
    b = jnp.cumsum(log_a, axis=3)
    b_last = b[:, :, :, -1:, :]
    q_dec = q * jnp.exp(b)
    k_inv = k * jnp.exp(-b)
    k_end = k * jnp.exp(b_last - b)
    mask = jnp.tril(jnp.ones((GLA_CHUNK, GLA_CHUNK), dtype=bool))
    scores = jnp.where(mask, jnp.einsum('bhnck,bhnsk->bhncs', q_dec, k_inv), 0.0)
    o_intra = jnp.einsum('bhncs,bhnsv->bhncv', scores, v)
    d_state = jnp.einsum('bhnsk,bhnsv->bhnkv', k_end, v)
    decay = jnp.exp(b_last[:, :, :, 0, :])

    def step(state, inp):
        dec, ds = inp
        return dec[..., None] * state + ds, state

    s0 = jnp.zeros((B, H, dk, dv), jnp.float32)
    _, s_before = lax.scan(step, s0, (jnp.moveaxis(decay, 2, 0), jnp.moveaxis(d_state, 2, 0)))
    s_before = jnp.moveaxis(s_before, 0, 2)
    o_inter = jnp.einsum('bhnck,bhnkv->bhncv', q_dec, s_before)
    return (o_intra + o_inter).reshape(B, H, S, dv)


def hybrid_mixer(h, w_in, gla_a2, gla_a_b, conv_dw, conv_dw_b, conv_ln_g, conv_ln_b,
                 gla_norm_g, w_conv_out, w_sb_out, w_gla_out, w_o):
    B, S, _ = h.shape
    z = h @ w_in
    offs = []
    acc = 0
    for n in IN_SIZES[:-1]:
        acc += n
        offs.append(acc)
    u_conv, sb_qkv, gq, gk, gv, gg, g_lr, gates = jnp.split(z, offs, axis=-1)
    y_a = conformer_conv(u_conv, conv_dw, conv_dw_b, conv_ln_g, conv_ln_b, w_conv_out)
    sq, sk, sv = jnp.split(sb_qkv, 3, axis=-1)
    o_b = stick_breaking_attention(heads(sq, SB_HEADS), heads(sk, SB_HEADS), heads(sv, SB_HEADS))
    y_b = o_b.transpose(0, 2, 1, 3).reshape(B, S, SB_DIM) @ w_sb_out
    log_a = jax.nn.log_sigmoid((g_lr @ gla_a2 + gla_a_b).astype(jnp.float32)) / GLA_GATE_TAU
    o_c = gla_chunked(heads(gq, GLA_HEADS).astype(jnp.float32) * (GLA_DK ** -0.5),
                      heads(gk, GLA_HEADS).astype(jnp.float32),
                      heads(gv, GLA_HEADS).astype(jnp.float32),
                      heads(log_a, GLA_HEADS))
    o_c = rms_norm(o_c.transpose(0, 2, 1, 3), gla_norm_g.reshape(GLA_HEADS, GLA_DV)).astype(h.dtype)
    y_c = (o_c.reshape(B, S, GLA_VDIM) * jax.nn.silu(gg)) @ w_gla_out
    g_a, g_b, g_c = jnp.split(jax.nn.sigmoid(gates), N_BRANCH, axis=-1)
    return (g_a * y_a + g_b * y_b + g_c * y_c) @ w_o


def swiglu(h, w_gate, w_up, w_down):
    return (jax.nn.silu(h @ w_gate) * (h @ w_up)) @ w_down


def moe_swiglu(h, router_w, w_gate, w_up, w_down):
    logits = (h @ router_w).astype(jnp.float32)
    top_val, top_idx = lax.top_k(logits, TOP_K)
    probs = jax.nn.softmax(top_val, axis=-1)
    combine = jnp.sum(jax.nn.one_hot(top_idx, N_EXPERTS, dtype=jnp.float32) * probs[..., None], axis=-2)
    out = jnp.zeros_like(h)
    for e in range(N_EXPERTS):
        out = out + combine[..., e:e + 1].astype(h.dtype) * swiglu(h, w_gate[e], w_up[e], w_down[e])
    return out


def setup_inputs(seed: int = 0) -> dict:
    key = jax.random.key(seed)
    ks = iter(jax.random.split(key, 40))

    def nrm(shape, scale):
        return jax.random.normal(next(ks), shape, jnp.float32) * scale

    def gain(shape):
        return 1.0 + nrm(shape, 0.02)

    L = DEPTH
    ND = (DEPTH + 1) // 2
    NM = DEPTH // 2
    return {
        'x': nrm((BATCH, SEQ, D_MODEL), 1.0),
        'c': nrm((BATCH, D_MODEL), 1.0),
        'ada_w': nrm((L, D_MODEL, N_ADA * D_MODEL), 0.5 * D_MODEL ** -0.5),
        'ada_b': nrm((L, N_ADA * D_MODEL), 0.02),
        'mix_pre_g': gain((L, D_MODEL)),
        'mix_post_g': gain((L, D_MODEL)),
        'ffn_pre_g': gain((L, D_MODEL)),
        'ffn_post_g': gain((L, D_MODEL)),
        'w_in': nrm((L, D_MODEL, IN_COLS), D_MODEL ** -0.5),
        'gla_a2': nrm((L, GLA_GATE_RANK, GLA_KDIM), GLA_GATE_RANK ** -0.5),
        'gla_a_b': nrm((L, GLA_KDIM), 0.02),
        'conv_dw': nrm((L, CONV_WIDTH, CONV_DIM), CONV_WIDTH ** -0.5),
        'conv_dw_b': nrm((L, CONV_DIM), 0.02),
        'conv_ln_g': gain((L, CONV_DIM)),
        'conv_ln_b': nrm((L, CONV_DIM), 0.02),
        'gla_norm_g': gain((L, GLA_VDIM)),
        'w_conv_out': nrm((L, CONV_DIM, D_MODEL), CONV_DIM ** -0.5),
        'w_sb_out': nrm((L, SB_DIM, D_MODEL), SB_DIM ** -0.5),
        'w_gla_out': nrm((L, GLA_VDIM, D_MODEL), GLA_VDIM ** -0.5),
        'w_o': nrm((L, D_MODEL, D_MODEL), D_MODEL ** -0.5),
        'ffn_w_gate': nrm((ND, D_MODEL, D_FF), D_MODEL ** -0.5),
        'ffn_w_up': nrm((ND, D_MODEL, D_FF), D_MODEL ** -0.5),
        'ffn_w_down': nrm((ND, D_FF, D_MODEL), D_FF ** -0.5),
        'router_w': nrm((NM, D_MODEL, N_EXPERTS), D_MODEL ** -0.5),
        'moe_w_gate': nrm((NM, N_EXPERTS, D_MODEL, D_FF_EXPERT), D_MODEL ** -0.5),
        'moe_w_up': nrm((NM, N_EXPERTS, D_MODEL, D_FF_EXPERT), D_MODEL ** -0.5),
        'moe_w_down': nrm((NM, N_EXPERTS, D_FF_EXPERT, D_MODEL), D_FF_EXPERT ** -0.5),
    }


def reference(x, c, ada_w, ada_b, mix_pre_g, mix_post_g, ffn_pre_g, ffn_post_g, w_in,
              gla_a2, gla_a_b, conv_dw, conv_dw_b, conv_ln_g, conv_ln_b, gla_norm_g,
              w_conv_out, w_sb_out, w_gla_out, w_o, ffn_w_gate, ffn_w_up, ffn_w_down,
              router_w, moe_w_gate, moe_w_up, moe_w_down):
    cond = jax.nn.silu(c)
    for l in range(DEPTH):
        mod = (cond @ ada_w[l] + ada_b[l])[:, None, :]
        sh_m, sc_m, g_m, sh_f, sc_f, g_f = jnp.split(mod, N_ADA, axis=-1)
        h = rms_norm(x, mix_pre_g[l]) * (1 + sc_m) + sh_m
        y = hybrid_mixer(h, w_in[l], gla_a2[l], gla_a_b[l], conv_dw[l], conv_dw_b[l],
                         conv_ln_g[l], conv_ln_b[l], gla_norm_g[l], w_conv_out[l],
                         w_sb_out[l], w_gla_out[l], w_o[l])
        x = x + g_m * rms_norm(y, mix_post_g[l])
        h = rms_norm(x, ffn_pre_g[l]) * (1 + sc_f) + sh_f
        j = l // 2
        if l % 2 == 0:
            y = swiglu(h, ffn_w_gate[j], ffn_w_up[j], ffn_w_down[j])
        else:
            y = moe_swiglu(h, router_w[j], moe_w_gate[j], moe_w_up[j], moe_w_down[j])
        x = x + g_f * rms_norm(y, ffn_post_g[l])
    return x
```

```python
import functools

import jax
import jax.numpy as jnp
from jax import lax
from jax.experimental import pallas as pl
from jax.experimental.pallas import tpu as pltpu

BF = jnp.bfloat16
F32 = jnp.float32
I32 = jnp.int32

EPS = 1e-6
N_ADA = 6
CONV_DIM = 512
CONV_WIDTH = 31
CONV_HALO = 32
SB_HEADS = 8
SB_HEAD_DIM = 64
SB_DIM = SB_HEADS * SB_HEAD_DIM
SB_BLOCK = 128
GLA_HEADS = 4
GLA_DK = 128
GLA_DV = 128
GLA_DIM = GLA_HEADS * GLA_DK
GLA_RANK = 16
GLA_TAU = 16.0
GLA_CHUNK = 64
N_EXPERTS = 8
LANES = 128
VMEM_LIMIT = 56 * 1024 * 1024

TM_IN = 256
TS_CONV = 512
T_GLA = 512
TM_OUT = 512
TM_FFN = 512
TM_MOE = 512
TG_MOE = 256
CK_MOE = 256


def _cparams(sem):
    return pltpu.CompilerParams(dimension_semantics=sem, vmem_limit_bytes=VMEM_LIMIT)


def _dot(a, b):
    return jnp.dot(a, b, preferred_element_type=F32)


def _dot_nt(a, b):
    return lax.dot_general(a, b, (((1,), (1,)), ((), ())), preferred_element_type=F32)


def _dot_tn(a, b):
    return lax.dot_general(a, b, (((0,), (0,)), ((), ())), preferred_element_type=F32)


def _sigmoid(x):
    return 1.0 / (1.0 + jnp.exp(-x))


def _log_sigmoid(x):
    return jnp.minimum(x, 0.0) - jnp.log(1.0 + jnp.exp(-jnp.abs(x)))


def _rms(y):
    return y * lax.rsqrt(jnp.mean(y * y, axis=-1, keepdims=True) + EPS)


def _split_bf16(x):
    hi = x.astype(BF)
    lo = (x - hi.astype(F32)).astype(BF)
    return hi, lo


def _ada_kernel(c_ref, w_ref, b_ref, o_ref):
    c = c_ref[...]
    cond = c * _sigmoid(c)
    o_ref[0] = _dot(cond.astype(BF), w_ref[0].astype(BF)) + b_ref[0]


def _ada_mod(c, ada_w, ada_b):
    L, D, D6 = ada_w.shape
    B = c.shape[0]
    rows = 8
    cp = jnp.zeros((rows, D), F32).at[:B].set(c)
    tn = 1536
    out = pl.pallas_call(
        _ada_kernel,
        grid=(L, D6 // tn),
        in_specs=[
            pl.BlockSpec((rows, D), lambda l, j: (0, 0)),
            pl.BlockSpec((1, D, tn), lambda l, j: (l, 0, j)),
            pl.BlockSpec((1, 1, tn), lambda l, j: (l, 0, j)),
        ],
        out_specs=pl.BlockSpec((1, rows, tn), lambda l, j: (l, 0, j)),
        out_shape=jax.ShapeDtypeStruct((L, rows, D6), F32),
        compiler_params=_cparams(("parallel", "parallel")),
        name="ada_mod",
    )(cp, ada_w, ada_b.reshape(L, 1, D6))
    mod = out[:, :B].reshape(L, B, N_ADA, D).transpose(0, 2, 1, 3)
    return mod.reshape(L * N_ADA * B, 1, D)


def _mod_spec(l, k, B, D, batch_of):
    base = (l * N_ADA + k) * B
    return pl.BlockSpec((1, 1, D), lambda *g: (base + batch_of(*g), 0, 0))


def _prenorm(x, g, sc, sh):
    return _rms(x) * g * (1.0 + sc) + sh


C_CONV = 2 * CONV_DIM
C_SB = 3 * SB_DIM
C_GLA = 4 * GLA_DIM + LANES
C_GATE = 3 * 1024


def _mixer_in_kernel(x_ref, sc_ref, sh_ref, g_ref, w_ref, glu_ref, sb_ref, gla_ref, gate_ref):
    h = _prenorm(x_ref[0], g_ref[...], sc_ref[0], sh_ref[0]).astype(BF)
    o0 = 0
    u = _dot(h, w_ref[:, o0:o0 + C_CONV])
    glu_ref[0] = (u[:, :CONV_DIM] * _sigmoid(u[:, CONV_DIM:])).astype(BF)
    o0 += C_CONV
    sb_ref[0] = _dot(h, w_ref[:, o0:o0 + C_SB]).astype(BF)
    o0 += C_SB
    nq = 3 * GLA_DIM
    gla_ref[0, :, 0:nq] = _dot(h, w_ref[:, o0:o0 + nq]).astype(BF)
    gg = _dot(h, w_ref[:, o0 + nq:o0 + nq + GLA_DIM])
    gla_ref[0, :, nq:nq + GLA_DIM] = (gg * _sigmoid(gg)).astype(BF)
    gla_ref[0, :, nq + GLA_DIM:] = _dot(h, w_ref[:, o0 + nq + GLA_DIM:o0 + C_GLA]).astype(BF)
    o0 += C_GLA
    gate_ref[0] = _sigmoid(_dot(h, w_ref[:, o0:o0 + C_GATE])).astype(BF)


def _mixer_in(x, modr, l, pre_g, w_in_p):
    B, S, D = x.shape
    tm = min(TM_IN, S)
    NC = w_in_p.shape[1]
    bof = lambda b, i: b
    outs = pl.pallas_call(
        _mixer_in_kernel,
        grid=(B, S // tm),
        in_specs=[
            pl.BlockSpec((1, tm, D), lambda b, i: (b, i, 0)),
            _mod_spec(l, 1, B, D, bof),
            _mod_spec(l, 0, B, D, bof),
            pl.BlockSpec((1, D), lambda b, i: (0, 0)),
            pl.BlockSpec((D, NC), lambda b, i: (0, 0)),
        ],
        out_specs=[
            pl.BlockSpec((1, tm, CONV_DIM), lambda b, i: (b, i, 0)),
            pl.BlockSpec((1, tm, C_SB), lambda b, i: (b, i, 0)),
            pl.BlockSpec((1, tm, C_GLA), lambda b, i: (b, i, 0)),
            pl.BlockSpec((1, tm, C_GATE), lambda b, i: (b, i, 0)),
        ],
        out_shape=[
            jax.ShapeDtypeStruct((B, S, CONV_DIM), BF),
            jax.ShapeDtypeStruct((B, S, C_SB), BF),
            jax.ShapeDtypeStruct((B, S, C_GLA), BF),
            jax.ShapeDtypeStruct((B, S, C_GATE), BF),
        ],
        compiler_params=_cparams(("parallel", "parallel")),
        name="mixer_in",
    )(x, modr, modr, pre_g.reshape(1, D), w_in_p)
    return outs


def _conv_kernel(prev_ref, cur_ref, wdw_ref, bdw_ref, lng_ref, lnb_ref, o_ref, ycat_ref, *, ts):
    i = pl.program_id(1)
    prev = prev_ref[0].astype(F32)
    ycat_ref[0:CONV_HALO, :] = jnp.where(i > 0, prev, 0.0)
    ycat_ref[CONV_HALO:, :] = cur_ref[0].astype(F32)
    rc = 64
    first = CONV_HALO - (CONV_WIDTH - 1)
    for r0 in range(0, ts, rc):
        acc = jnp.zeros((rc, CONV_DIM), F32) + bdw_ref[...]
        for w in range(CONV_WIDTH):
            s0 = r0 + first + w
            acc = acc + wdw_ref[w:w + 1, :] * ycat_ref[s0:s0 + rc, :]
        mu = jnp.mean(acc, axis=-1, keepdims=True)
        d = acc - mu
        var = jnp.mean(d * d, axis=-1, keepdims=True)
        y = d * lax.rsqrt(var + EPS) * lng_ref[...] + lnb_ref[...]
        o_ref[0, r0:r0 + rc, :] = (y * _sigmoid(y)).astype(BF)


def _conv_branch(glu, wdw, bdw, lng, lnb):
    B, S, C = glu.shape
    ts = min(TS_CONV, S)
    hb = ts // CONV_HALO
    vec = lambda b, i: (0, 0)
    return pl.pallas_call(
        functools.partial(_conv_kernel, ts=ts),
        grid=(B, S // ts),
        in_specs=[
            pl.BlockSpec((1, CONV_HALO, C), lambda b, i: (b, jnp.maximum(i * hb - 1, 0), 0)),
            pl.BlockSpec((1, ts, C), lambda b, i: (b, i, 0)),
            pl.BlockSpec((CONV_WIDTH, C), vec),
            pl.BlockSpec((1, C), vec),
            pl.BlockSpec((1, C), vec),
            pl.BlockSpec((1, C), vec),
        ],
        out_specs=pl.BlockSpec((1, ts, C), lambda b, i: (b, i, 0)),
        out_shape=jax.ShapeDtypeStruct((B, S, C), BF),
        scratch_shapes=[pltpu.VMEM((ts + CONV_HALO, C), F32)],
        compiler_params=_cparams(("parallel", "parallel")),
        name="conv_branch",
    )(glu, glu, wdw, bdw.reshape(1, C), lng.reshape(1, C), lnb.reshape(1, C))


def _sb_kernel(q_ref, k_ref, v_ref, uu_ref, o_ref, acc_ref, carry_ref):
    i = pl.program_id(2)
    tb = SB_BLOCK
    scale = SB_HEAD_DIM ** -0.5
    lane = lax.broadcasted_iota(I32, (1, LANES), 1)
    head_masks = (lane < SB_HEAD_DIM, lane >= SB_HEAD_DIM)
    q = q_ref[0]
    qz = jnp.zeros_like(q)
    qh = [jnp.where(m, q, qz) for m in head_masks]
    uu = uu_ref[...]
    acc_ref[...] = jnp.zeros_like(acc_ref)
    carry_ref[...] = jnp.zeros_like(carry_ref)
    row = lax.broadcasted_iota(I32, (tb, tb), 0)
    col = lax.broadcasted_iota(I32, (tb, tb), 1)
    causal = col < row

    def block(kb, masked):
        start = pl.multiple_of(kb * tb, tb)
        k = k_ref[0, pl.ds(start, tb), :]
        v = v_ref[0, pl.ds(start, tb), :]
        vz = jnp.zeros_like(v)
        vv = jnp.concatenate([jnp.where(m, v, vz) for m in head_masks], axis=0)
        atts = []
        for h in range(2):
            z = _dot_nt(qh[h], k) * scale
            lb = _log_sigmoid(z)
            lk = lb - z
            if masked:
                lk = jnp.where(causal, lk, 0.0)
            hi, lo = _split_bf16(lk)
            r = _dot(jnp.concatenate([hi, lo], axis=1), uu)
            att = jnp.exp(lb + r[:, :tb] + carry_ref[h])
            if masked:
                att = jnp.where(causal, att, 0.0)
            carry_ref[h] = carry_ref[h] + r[:, tb:]
            atts.append(att.astype(BF))
        acc_ref[...] += _dot(jnp.concatenate(atts, axis=1), vv)

    block(i, True)

    def body(j, c):
        block(i - 1 - j, False)
        return c

    lax.fori_loop(0, i, body, 0)
    o_ref[0] = acc_ref[...].astype(BF)


def _sb_consts():
    tb = SB_BLOCK
    j = jnp.arange(tb)[:, None]
    s = jnp.arange(tb)[None, :]
    u = (j > s).astype(BF)
    half = jnp.concatenate([u, jnp.ones((tb, tb), BF)], axis=1)
    return jnp.concatenate([half, half], axis=0)


def _sb_attention(sb):
    B, S, _ = sb.shape
    tb = SB_BLOCK
    npair = SB_DIM // LANES
    return pl.pallas_call(
        _sb_kernel,
        grid=(B, npair, S // tb),
        in_specs=[
            pl.BlockSpec((1, tb, LANES), lambda b, p, i: (b, i, p)),
            pl.BlockSpec((1, S, LANES), lambda b, p, i: (b, 0, npair + p)),
            pl.BlockSpec((1, S, LANES), lambda b, p, i: (b, 0, 2 * npair + p)),
            pl.BlockSpec((2 * tb, 2 * tb), lambda b, p, i: (0, 0)),
        ],
        out_specs=pl.BlockSpec((1, tb, LANES), lambda b, p, i: (b, i, p)),
        out_shape=jax.ShapeDtypeStruct((B, S, SB_DIM), BF),
        scratch_shapes=[pltpu.VMEM((tb, LANES), F32), pltpu.VMEM((2, tb, LANES), F32)],
        compiler_params=_cparams(("parallel", "parallel", "parallel")),
        name="sb_attention",
    )(sb, sb, sb, _sb_consts())


def _gla_kernel(q_ref, k_ref, v_ref, gg_ref, glr_ref, a2_ref, ab_ref, ng_ref, lt_ref, o_ref,
                st_ref, *, t):
    @pl.when(pl.program_id(2) == 0)
    def _():
        st_ref[...] = jnp.zeros_like(st_ref)

    cs = GLA_CHUNK
    scale = GLA_DK ** -0.5
    lt = lt_ref[...]
    row = lax.broadcasted_iota(I32, (cs, cs), 0)
    col = lax.broadcasted_iota(I32, (cs, cs), 1)
    tril = col <= row
    for c in range(t // cs):
        sl = slice(c * cs, (c + 1) * cs)
        u = _dot(glr_ref[0, sl, :], a2_ref[...]) + ab_ref[...]
        la = _log_sigmoid(u) * (1.0 / GLA_TAU)
        hi, lo = _split_bf16(la)
        b = _dot(lt, jnp.concatenate([hi, lo], axis=0))
        bl = b[cs - 1:cs, :]
        q = q_ref[0, sl, :].astype(F32) * scale
        k = k_ref[0, sl, :].astype(F32)
        v = v_ref[0, sl, :]
        qd = (q * jnp.exp(b)).astype(BF)
        ki = (k * jnp.exp(-b)).astype(BF)
        ke = (k * jnp.exp(bl - b)).astype(BF)
        sc = jnp.where(tril, _dot_nt(qd, ki), 0.0).astype(BF)
        st = st_ref[...]
        o = _dot(sc, v) + _dot_nt(qd, st.astype(BF))
        st_ref[...] = st * jnp.exp(bl) + _dot_tn(v, ke)
        y = _rms(o) * ng_ref[...] * gg_ref[0, sl, :].astype(F32)
        o_ref[0, sl, :] = y.astype(BF)


def _gla_consts():
    cs = GLA_CHUNK
    r = jnp.arange(cs)[:, None]
    c = jnp.arange(cs)[None, :]
    low = (c <= r).astype(BF)
    return jnp.concatenate([low, low], axis=1)


def _gla_branch(gla, a2p, ab, ng):
    B, S, _ = gla.shape
    t = min(T_GLA, S)
    H = GLA_HEADS
    col = lambda off: (lambda b, h, s: (b, s, off + h))
    par = lambda b, h, s: (0, h)
    return pl.pallas_call(
        functools.partial(_gla_kernel, t=t),
        grid=(B, H, S // t),
        in_specs=[
            pl.BlockSpec((1, t, LANES), col(0)),
            pl.BlockSpec((1, t, LANES), col(H)),
            pl.BlockSpec((1, t, LANES), col(2 * H)),
            pl.BlockSpec((1, t, LANES), col(3 * H)),
            pl.BlockSpec((1, t, LANES), lambda b, h, s: (b, s, 4 * H)),
            pl.BlockSpec((LANES, GLA_DK), par),
            pl.BlockSpec((1, GLA_DK), par),
            pl.BlockSpec((1, GLA_DV), par),
            pl.BlockSpec((GLA_CHUNK, 2 * GLA_CHUNK), lambda b, h, s: (0, 0)),
        ],
        out_specs=pl.BlockSpec((1, t, LANES), lambda b, h, s: (b, s, h)),
        out_shape=jax.ShapeDtypeStruct((B, S, GLA_DIM), BF),
        scratch_shapes=[pltpu.VMEM((GLA_DV, GLA_DK), F32)],
        compiler_params=_cparams(("parallel", "parallel", "arbitrary")),
        name="gla_branch",
    )(gla, gla, gla, gla, gla, a2p, ab.reshape(1, GLA_DIM), ng.reshape(1, GLA_DIM), _gla_consts())


def _mixer_out_kernel(x_ref, ca_ref, ob_ref, oc_ref, gt_ref, wc_ref, wb_ref, wg_ref, wo_ref,
                      gm_ref, pg_ref, o_ref):
    D = x_ref.shape[-1]
    gt = gt_ref[0]
    m = gt[:, 0:D].astype(F32) * _dot(ca_ref[0], wc_ref[...])
    m = m + gt[:, D:2 * D].astype(F32) * _dot(ob_ref[0], wb_ref[...])
    m = m + gt[:, 2 * D:3 * D].astype(F32) * _dot(oc_ref[0], wg_ref[...])
    y = _dot(m.astype(BF), wo_ref[...])
    o_ref[0] = x_ref[0] + gm_ref[0] * (_rms(y) * pg_ref[...])


def _mixer_out(x, ca, ob, oc, gates, wc, wb, wg, wo, modr, l, post_g):
    B, S, D = x.shape
    tm = min(TM_OUT, S)
    tok = lambda w: pl.BlockSpec((1, tm, w), lambda b, i: (b, i, 0))
    full = lambda a: pl.BlockSpec(a.shape, lambda b, i: (0, 0))
    return pl.pallas_call(
        _mixer_out_kernel,
        grid=(B, S // tm),
        in_specs=[tok(D), tok(CONV_DIM), tok(SB_DIM), tok(GLA_DIM), tok(C_GATE),
                  full(wc), full(wb), full(wg), full(wo),
                  _mod_spec(l, 2, B, D, lambda b, i: b),
                  pl.BlockSpec((1, D), lambda b, i: (0, 0))],
        out_specs=tok(D),
        out_shape=jax.ShapeDtypeStruct((B, S, D), F32),
        compiler_params=_cparams(("parallel", "parallel")),
        name="mixer_out",
    )(x, ca, ob, oc, gates, wc, wb, wg, wo, modr, post_g.reshape(1, D))


def _ffn_kernel(x_ref, sc_ref, sh_ref, pre_ref, wg_ref, wu_ref, wd_ref, gf_ref, post_ref, o_ref,
                h_ref, acc_ref):
    j = pl.program_id(2)

    @pl.when(j == 0)
    def _():
        h_ref[...] = _prenorm(x_ref[0], pre_ref[...], sc_ref[0], sh_ref[0]).astype(BF)
        acc_ref[...] = jnp.zeros_like(acc_ref)

    h = h_ref[...]
    g = _dot(h, wg_ref[...])
    u = _dot(h, wu_ref[...])
    acc_ref[...] += _dot((g * _sigmoid(g) * u).astype(BF), wd_ref[...])

    @pl.when(j == pl.num_programs(2) - 1)
    def _():
        o_ref[0] = x_ref[0] + gf_ref[0] * (_rms(acc_ref[...]) * post_ref[...])


def _ffn_dense(x, modr, l, pre_g, post_g, wg, wu, wd):
    B, S, D = x.shape
    F = wg.shape[1]
    tm = min(TM_FFN, S)
    tf = F // 2 if (F // 2) % LANES == 0 else F
    bof = lambda b, i, j: b
    return pl.pallas_call(
        _ffn_kernel,
        grid=(B, S // tm, F // tf),
        in_specs=[
            pl.BlockSpec((1, tm, D), lambda b, i, j: (b, i, 0)),
            _mod_spec(l, 4, B, D, bof),
            _mod_spec(l, 3, B, D, bof),
            pl.BlockSpec((1, D), lambda b, i, j: (0, 0)),
            pl.BlockSpec((D, tf), lambda b, i, j: (0, j)),
            pl.BlockSpec((D, tf), lambda b, i, j: (0, j)),
            pl.BlockSpec((tf, D), lambda b, i, j: (j, 0)),
            _mod_spec(l, 5, B, D, bof),
            pl.BlockSpec((1, D), lambda b, i, j: (0, 0)),
        ],
        out_specs=pl.BlockSpec((1, tm, D), lambda b, i, j: (b, i, 0)),
        out_shape=jax.ShapeDtypeStruct((B, S, D), F32),
        scratch_shapes=[pltpu.VMEM((tm, D), BF), pltpu.VMEM((tm, D), F32)],
        compiler_params=_cparams(("parallel", "parallel", "arbitrary")),
        name="ffn_dense",
    )(x, modr, modr, pre_g.reshape(1, D), wg, wu, wd, modr, post_g.reshape(1, D))


def _router_kernel(x_ref, sc_ref, sh_ref, pre_ref, rw_ref, h_ref, comb_ref):
    h = _prenorm(x_ref[0], pre_ref[...], sc_ref[0], sh_ref[0])
    h_ref[0] = h.astype(BF)
    hh, hl = _split_bf16(h)
    rh, rl = _split_bf16(rw_ref[...])
    lg = _dot(hh, rh) + _dot(hh, rl) + _dot(hl, rh)
    lane = lax.broadcasted_iota(I32, lg.shape, 1).astype(F32)
    ninf = jnp.float32(-jnp.inf)
    lg = jnp.where(lane < N_EXPERTS, lg, ninf)
    m1 = jnp.max(lg, axis=-1, keepdims=True)
    i1 = jnp.min(jnp.where(lg == m1, lane, float(LANES)), axis=-1, keepdims=True)
    lg2 = jnp.where(lane == i1, ninf, lg)
    m2 = jnp.max(lg2, axis=-1, keepdims=True)
    i2 = jnp.min(jnp.where(lg2 == m2, lane, float(LANES)), axis=-1, keepdims=True)
    e = jnp.exp(m2 - m1)
    p1 = 1.0 / (1.0 + e)
    comb_ref[0] = jnp.where(lane == i1, p1, jnp.where(lane == i2, e * p1, -1.0))


def _router(x, modr, l, pre_g, rw_p):
    B, S, D = x.shape
    tm = min(TM_FFN, S)
    bof = lambda b, i: b
    return pl.pallas_call(
        _router_kernel,
        grid=(B, S // tm),
        in_specs=[
            pl.BlockSpec((1, tm, D), lambda b, i: (b, i, 0)),
            _mod_spec(l, 4, B, D, bof),
            _mod_spec(l, 3, B, D, bof),
            pl.BlockSpec((1, D), lambda b, i: (0, 0)),
            pl.BlockSpec((D, LANES), lambda b, i: (0, 0)),
        ],
        out_specs=[pl.BlockSpec((1, tm, D), lambda b, i: (b, i, 0)),
                   pl.BlockSpec((1, tm, LANES), lambda b, i: (b, i, 0))],
        out_shape=[jax.ShapeDtypeStruct((B, S, D), BF), jax.ShapeDtypeStruct((B, S, LANES), F32)],
        compiler_params=_cparams(("parallel", "parallel")),
        name="moe_router",
    )(x, modr, modr, pre_g.reshape(1, D), rw_p)


def _moe_plan(comb, n_tok):
    E = N_EXPERTS
    nck = n_tok // CK_MOE
    sel = comb[:, :E] >= 0.0
    seli = sel.astype(I32)
    rank_incl = jnp.cumsum(seli, axis=0)
    cnt = rank_incl[-1]
    padded = ((cnt + TM_MOE - 1) // TM_MOE) * TM_MOE
    seg_end = jnp.cumsum(padded)
    seg_start = seg_end - padded
    total = seg_end[-1]
    pos = jnp.where(sel, seg_start[None, :] + rank_incl - seli, -1)
    pos_ck = pos.reshape(nck, CK_MOE, E).transpose(0, 2, 1)
    comb_ck = jnp.where(sel, comb[:, :E], 0.0).reshape(nck, CK_MOE, E).transpose(0, 2, 1)
    cnt_ck = seli.reshape(nck, CK_MOE, E).sum(axis=1)
    cex = jnp.concatenate([jnp.zeros((1, E), I32), jnp.cumsum(cnt_ck, axis=0)], axis=0)

    rows_pad = 2 * n_tok + E * TM_MOE
    jt = jnp.arange(rows_pad // TG_MOE, dtype=I32) * TG_MOE
    te = jnp.minimum(jnp.sum(seg_end[None, :] <= jt[:, None], axis=1), E - 1).astype(I32)
    r0 = jt - seg_start[te]
    r1 = jnp.minimum(r0 + TG_MOE, cnt[te])
    valid = (jt < total) & (r1 > r0)
    cte = cex[1:, :][:, te]
    c_lo = jnp.sum(cte <= r0[None, :], axis=0).astype(I32)
    c_hi = jnp.sum(cte <= (r1 - 1)[None, :], axis=0).astype(I32)
    c_lo = jnp.where(valid, c_lo, 1)
    c_hi = jnp.where(valid, jnp.minimum(c_hi, nck - 1), 0)
    jf = jnp.arange(rows_pad // TM_MOE, dtype=I32) * TM_MOE
    fe = jnp.minimum(jnp.sum(seg_end[None, :] <= jf[:, None], axis=1), E - 1).astype(I32)
    fvalid = (jf < total).astype(I32)
    w0 = seg_start[None, :] + cex[:-1, :]
    nb = rows_pad // CK_MOE
    b0 = jnp.minimum(w0 // CK_MOE, nb - 1).astype(I32)
    b1 = jnp.minimum(b0 + 1, nb - 1).astype(I32)
    dup = (b1 == b0).astype(I32)
    return dict(pos_ck=pos_ck, comb_ck=comb_ck, te=te, c_lo=c_lo, c_hi=c_hi, fe=fe, fvalid=fvalid,
                b0=b0.reshape(-1), b1=b1.reshape(-1), dup=dup.reshape(-1), rows_pad=rows_pad)


def _gather_kernel(te_ref, clo_ref, chi_ref, h_ref, pos_ref, o_ref, acc_ref):
    j = pl.program_id(0)
    e = te_ref[j]
    rows = j * TG_MOE + lax.broadcasted_iota(I32, (TG_MOE, CK_MOE), 0)
    acc_ref[...] = jnp.zeros_like(acc_ref)

    def body(c, carry):
        p = pos_ref[c, pl.ds(e, 1), :]
        onehot = jnp.where(rows == p, 1.0, 0.0).astype(BF)
        start = pl.multiple_of(c * CK_MOE, CK_MOE)
        acc_ref[...] += _dot(onehot, h_ref[pl.ds(start, CK_MOE), :])
        return carry

    lax.fori_loop(clo_ref[j], chi_ref[j] + 1, body, 0)
    o_ref[...] = acc_ref[...].astype(BF)


def _moe_gather(h2, plan):
    n_tok, D = h2.shape
    rows_pad = plan["rows_pad"]
    grid_spec = pltpu.PrefetchScalarGridSpec(
        num_scalar_prefetch=3,
        grid=(rows_pad // TG_MOE,),
        in_specs=[pl.BlockSpec(memory_space=pltpu.VMEM), pl.BlockSpec(memory_space=pltpu.VMEM)],
        out_specs=pl.BlockSpec((TG_MOE, D), lambda j, *_: (j, 0)),
        scratch_shapes=[pltpu.VMEM((TG_MOE, D), F32)],
    )
    return pl.pallas_call(
        _gather_kernel,
        grid_spec=grid_spec,
        out_shape=jax.ShapeDtypeStruct((rows_pad, D), BF),
        compiler_params=_cparams(("arbitrary",)),
        name="moe_gather",
    )(plan["te"], plan["c_lo"], plan["c_hi"], h2, plan["pos_ck"])


def _moe_ffn_kernel(fe_ref, fv_ref, x_ref, wg_ref, wu_ref, wd_ref, o_ref, acc_ref):
    j = pl.program_id(0)
    f = pl.program_id(1)

    @pl.when(f == 0)
    def _():
        acc_ref[...] = jnp.zeros_like(acc_ref)

    @pl.when(fv_ref[j] == 1)
    def _():
        x = x_ref[...]
        g = _dot(x, wg_ref[0])
        u = _dot(x, wu_ref[0])
        acc_ref[...] += _dot((g * _sigmoid(g) * u).astype(BF), wd_ref[0])

    @pl.when(f == pl.num_programs(1) - 1)
    def _():
        o_ref[...] = acc_ref[...].astype(BF)


def _moe_ffn(xs, plan, wg, wu, wd):
    rows_pad, D = xs.shape
    F = wg.shape[2]
    tf = F // 2 if (F // 2) % LANES == 0 else F
    nf = F // tf
    fidx = lambda f, fv, j: f * fv[j] + (nf - 1) * (1 - fv[j])
    grid_spec = pltpu.PrefetchScalarGridSpec(
        num_scalar_prefetch=2,
        grid=(rows_pad // TM_MOE, nf),
        in_specs=[
            pl.BlockSpec((TM_MOE, D), lambda j, f, fe, fv: (j, 0)),
            pl.BlockSpec((1, D, tf), lambda j, f, fe, fv: (fe[j], 0, fidx(f, fv, j))),
            pl.BlockSpec((1, D, tf), lambda j, f, fe, fv: (fe[j], 0, fidx(f, fv, j))),
            pl.BlockSpec((1, tf, D), lambda j, f, fe, fv: (fe[j], fidx(f, fv, j), 0)),
        ],
        out_specs=pl.BlockSpec((TM_MOE, D), lambda j, f, fe, fv: (j, 0)),
        scratch_shapes=[pltpu.VMEM((TM_MOE, D), F32)],
    )
    return pl.pallas_call(
        _moe_ffn_kernel,
        grid_spec=grid_spec,
        out_shape=jax.ShapeDtypeStruct((rows_pad, D), BF),
        compiler_params=_cparams(("arbitrary", "arbitrary")),
        name="moe_ffn",
    )(plan["fe"], plan["fvalid"], xs, wg, wu, wd)


def _combine_kernel(b0_ref, b1_ref, dup_ref, y0_ref, y1_ref, pos_ref, comb_ref, x_ref, gf_ref,
                    post_ref, o_ref, acc_ref):
    c = pl.program_id(0)
    e = pl.program_id(1)

    @pl.when(e == 0)
    def _():
        acc_ref[...] = jnp.zeros_like(acc_ref)

    idx = c * N_EXPERTS + e
    p = pos_ref[0, pl.ds(e, 1), :]
    w = comb_ref[0, pl.ds(e, 1), :]
    r = lax.broadcasted_iota(I32, (CK_MOE, CK_MOE), 0)
    w0 = jnp.where(p == b0_ref[idx] * CK_MOE + r, w, 0.0).astype(BF)
    keep = (1 - dup_ref[idx]).astype(F32)
    w1 = jnp.where(p == b1_ref[idx] * CK_MOE + r, w * keep, 0.0).astype(BF)
    acc_ref[...] += _dot_tn(w0, y0_ref[...]) + _dot_tn(w1, y1_ref[...])

    @pl.when(e == N_EXPERTS - 1)
    def _():
        o_ref[...] = x_ref[...] + gf_ref[0] * (_rms(acc_ref[...]) * post_ref[...])


def _moe_combine(x2, ys, plan, modr, l, post_g, B):
    n_tok, D = x2.shape
    nck = n_tok // CK_MOE
    ck_per_b = nck // B
    E = N_EXPERTS
    base = (l * N_ADA + 5) * B
    grid_spec = pltpu.PrefetchScalarGridSpec(
        num_scalar_prefetch=3,
        grid=(nck, E),
        in_specs=[
            pl.BlockSpec((CK_MOE, D), lambda c, e, b0, b1, du: (b0[c * E + e], 0)),
            pl.BlockSpec((CK_MOE, D), lambda c, e, b0, b1, du: (b1[c * E + e], 0)),
            pl.BlockSpec((1, E, CK_MOE), lambda c, e, *_: (c, 0, 0)),
            pl.BlockSpec((1, E, CK_MOE), lambda c, e, *_: (c, 0, 0)),
            pl.BlockSpec((CK_MOE, D), lambda c, e, *_: (c, 0)),
            pl.BlockSpec((1, 1, D), lambda c, e, *_: (base + c // ck_per_b, 0, 0)),
            pl.BlockSpec((1, D), lambda c, e, *_: (0, 0)),
        ],
        out_specs=pl.BlockSpec((CK_MOE, D), lambda c, e, *_: (c, 0)),
        scratch_shapes=[pltpu.VMEM((CK_MOE, D), F32)],
    )
    return pl.pallas_call(
        _combine_kernel,
        grid_spec=grid_spec,
        out_shape=jax.ShapeDtypeStruct((n_tok, D), F32),
        compiler_params=_cparams(("arbitrary", "arbitrary")),
        name="moe_combine",
    )(plan["b0"], plan["b1"], plan["dup"], ys, ys, plan["pos_ck"], plan["comb_ck"], x2, modr,
      post_g.reshape(1, D))


def _moe_layer(x, modr, l, pre_g, post_g, router_w, wg, wu, wd):
    B, S, D = x.shape
    n_tok = B * S
    rw_p = jnp.zeros((D, LANES), F32).at[:, :N_EXPERTS].set(router_w)
    h, comb = _router(x, modr, l, pre_g, rw_p)
    plan = _moe_plan(comb.reshape(n_tok, LANES), n_tok)
    xs = _moe_gather(h.reshape(n_tok, D), plan)
    ys = _moe_ffn(xs, plan, wg, wu, wd)
    out = _moe_combine(x.reshape(n_tok, D), ys, plan, modr, l, post_g, B)
    return out.reshape(B, S, D)


def _pad_w_in(w_in_l):
    split = C_CONV + C_SB + 4 * GLA_DIM + GLA_RANK
    D = w_in_l.shape[0]
    pad = jnp.zeros((D, LANES - GLA_RANK), w_in_l.dtype)
    return jnp.concatenate([w_in_l[:, :split], pad, w_in_l[:, split:]], axis=1).astype(BF)


def kernel(x, c, ada_w, ada_b, mix_pre_g, mix_post_g, ffn_pre_g, ffn_post_g, w_in, gla_a2, gla_a_b, conv_dw, conv_dw_b, conv_ln_g, conv_ln_b, gla_norm_g, w_conv_out, w_sb_out, w_gla_out, w_o, ffn_w_gate, ffn_w_up, ffn_w_down, router_w, moe_w_gate, moe_w_up, moe_w_down):
    depth = w_in.shape[0]
    modr = _ada_mod(c, ada_w, ada_b)
    for l in range(depth):
        glu, sb, gla, gates = _mixer_in(x, modr, l, mix_pre_g[l], _pad_w_in(w_in[l]))
        ca = _conv_branch(glu, conv_dw[l], conv_dw_b[l], conv_ln_g[l], conv_ln_b[l])
        ob = _sb_attention(sb)
        a2p = jnp.zeros((LANES, GLA_DIM), F32).at[:GLA_RANK].set(gla_a2[l]).astype(BF)
        oc = _gla_branch(gla, a2p, gla_a_b[l], gla_norm_g[l])
        x = _mixer_out(x, ca, ob, oc, gates, w_conv_out[l].astype(BF), w_sb_out[l].astype(BF),
                       w_gla_out[l].astype(BF), w_o[l].astype(BF), modr, l, mix_post_g[l])
        j = l // 2
        if l % 2 == 0:
            x = _ffn_dense(x, modr, l, ffn_pre_g[l], ffn_post_g[l], ffn_w_gate[j].astype(BF),
                           ffn_w_up[j].astype(BF), ffn_w_down[j].astype(BF))
        else:
            x = _moe_layer(x, modr, l, ffn_pre_g[l], ffn_post_g[l], router_w[j],
                           moe_w_gate[j].astype(BF), moe_w_up[j].astype(BF),
                           moe_w_down[j].astype(BF))
    return x
```

```python
import functools

import jax
import jax.numpy as jnp
from jax import lax
from jax.experimental import pallas as pl
from jax.experimental.pallas import tpu as pltpu

BF = jnp.bfloat16
F32 = jnp.float32
I32 = jnp.int32

EPS = 1e-6
LOG2E = 1.4426950408889634
N_ADA = 6
CONV_DIM = 512
CONV_WIDTH = 31
CONV_HALO = 32
SB_HEADS = 8
SB_HEAD_DIM = 64
SB_DIM = SB_HEADS * SB_HEAD_DIM
SB_BLOCK = 128
SB_TQ = 512
GLA_HEADS = 4
GLA_DK = 128
GLA_DV = 128
GLA_DIM = GLA_HEADS * GLA_DK
GLA_RANK = 16
GLA_TAU = 16.0
GLA_CHUNK = 64
N_EXPERTS = 8
LANES = 128
VMEM_LIMIT = 56 * 1024 * 1024

TM_IN = 256
TS_CONV = 512
T_GLA = 512
TM_OUT = 512
TM_FFN = 512
TM_MOE = 1024
TF_MOE = 512
TG_MOE = 256
CK_MOE = 256


def _cparams(sem):
    return pltpu.CompilerParams(dimension_semantics=sem, vmem_limit_bytes=VMEM_LIMIT)


def _dot(a, b):
    return jnp.dot(a, b, preferred_element_type=F32)


def _dot_nt(a, b):
    return lax.dot_general(a, b, (((1,), (1,)), ((), ())), preferred_element_type=F32)


def _dot_tn(a, b):
    return lax.dot_general(a, b, (((0,), (0,)), ((), ())), preferred_element_type=F32)


def _sigmoid(x):
    return 1.0 / (1.0 + jnp.exp(-x))


def _log_sigmoid(x):
    return jnp.minimum(x, 0.0) - jnp.log(1.0 + jnp.exp2(jnp.abs(x) * (-LOG2E)))


def _rms(y):
    return y * lax.rsqrt(jnp.mean(y * y, axis=-1, keepdims=True) + EPS)


def _split_bf16(x):
    hi = x.astype(BF)
    lo = (x - hi.astype(F32)).astype(BF)
    return hi, lo


def _ada_kernel(c_ref, w_ref, b_ref, o_ref):
    c = c_ref[...]
    rows = c.shape[0]
    ch, cl = _split_bf16(c * _sigmoid(c))
    wh, wl = _split_bf16(w_ref[0])
    top = _dot(jnp.concatenate([ch, cl], axis=0), wh)
    o_ref[0] = top[:rows] + top[rows:] + _dot(ch, wl) + b_ref[0]


def _ada_mod(c, ada_w, ada_b):
    L, D, D6 = ada_w.shape
    B = c.shape[0]
    rows = 16
    cp = jnp.zeros((rows, D), F32).at[:B].set(c)
    tn = 1536
    out = pl.pallas_call(
        _ada_kernel,
        grid=(L, D6 // tn),
        in_specs=[
            pl.BlockSpec((rows, D), lambda l, j: (0, 0)),
            pl.BlockSpec((1, D, tn), lambda l, j: (l, 0, j)),
            pl.BlockSpec((1, 1, tn), lambda l, j: (l, 0, j)),
        ],
        out_specs=pl.BlockSpec((1, rows, tn), lambda l, j: (l, 0, j)),
        out_shape=jax.ShapeDtypeStruct((L, rows, D6), F32),
        compiler_params=_cparams(("parallel", "parallel")),
        name="ada_mod",
    )(cp, ada_w, ada_b.reshape(L, 1, D6))
    mod = out[:, :B].reshape(L, B, N_ADA, D).transpose(0, 2, 1, 3)
    return mod.reshape(L * N_ADA * B, 1, D)


def _mod_spec(l, k, B, D, batch_of):
    base = (l * N_ADA + k) * B
    return pl.BlockSpec((1, 1, D), lambda *g: (base + batch_of(*g), 0, 0))


def _prenorm(x, g, sc, sh):
    return _rms(x) * g * (1.0 + sc) + sh


C_CONV = 2 * CONV_DIM
C_SB = 3 * SB_DIM
C_GLA = 4 * GLA_DIM + LANES
C_GATE = 3 * 1024


def _mixer_in_kernel(x_ref, sc_ref, sh_ref, g_ref, w_ref, glu_ref, sb_ref, gla_ref, gate_ref):
    h = _prenorm(x_ref[0], g_ref[...], sc_ref[0], sh_ref[0]).astype(BF)
    o0 = 0
    u = _dot(h, w_ref[:, o0:o0 + C_CONV])
    glu_ref[0] = (u[:, :CONV_DIM] * _sigmoid(u[:, CONV_DIM:])).astype(BF)
    o0 += C_CONV
    sb_ref[0] = _dot(h, w_ref[:, o0:o0 + C_SB]).astype(BF)
    o0 += C_SB
    nq = 3 * GLA_DIM
    gla_ref[0, :, 0:nq] = _dot(h, w_ref[:, o0:o0 + nq]).astype(BF)
    gg = _dot(h, w_ref[:, o0 + nq:o0 + nq + GLA_DIM])
    gla_ref[0, :, nq:nq + GLA_DIM] = (gg * _sigmoid(gg)).astype(BF)
    gla_ref[0, :, nq + GLA_DIM:] = _dot(h, w_ref[:, o0 + nq + GLA_DIM:o0 + C_GLA]).astype(BF)
    o0 += C_GLA
    gate_ref[0] = _sigmoid(_dot(h, w_ref[:, o0:o0 + C_GATE])).astype(BF)


def _mixer_in(x, modr, l, pre_g, w_in_p):
    B, S, D = x.shape
    tm = min(TM_IN, S)
    NC = w_in_p.shape[1]
    bof = lambda b, i: b
    outs = pl.pallas_call(
        _mixer_in_kernel,
        grid=(B, S // tm),
        in_specs=[
            pl.BlockSpec((1, tm, D), lambda b, i: (b, i, 0)),
            _mod_spec(l, 1, B, D, bof),
            _mod_spec(l, 0, B, D, bof),
            pl.BlockSpec((1, D), lambda b, i: (0, 0)),
            pl.BlockSpec((D, NC), lambda b, i: (0, 0)),
        ],
        out_specs=[
            pl.BlockSpec((1, tm, CONV_DIM), lambda b, i: (b, i, 0)),
            pl.BlockSpec((1, tm, C_SB), lambda b, i: (b, i, 0)),
            pl.BlockSpec((1, tm, C_GLA), lambda b, i: (b, i, 0)),
            pl.BlockSpec((1, tm, C_GATE), lambda b, i: (b, i, 0)),
        ],
        out_shape=[
            jax.ShapeDtypeStruct((B, S, CONV_DIM), BF),
            jax.ShapeDtypeStruct((B, S, C_SB), BF),
            jax.ShapeDtypeStruct((B, S, C_GLA), BF),
            jax.ShapeDtypeStruct((B, S, C_GATE), BF),
        ],
        compiler_params=_cparams(("parallel", "parallel")),
        name="mixer_in",
    )(x, modr, modr, pre_g.reshape(1, D), w_in_p)
    return outs


def _conv_kernel(prev_ref, cur_ref, wdw_ref, bdw_ref, lng_ref, lnb_ref, o_ref, ycat_ref, *, ts):
    i = pl.program_id(1)
    prev = prev_ref[0].astype(F32)
    ycat_ref[0:CONV_HALO, :] = jnp.where(i > 0, prev, 0.0)
    ycat_ref[CONV_HALO:, :] = cur_ref[0].astype(F32)
    rc = 64
    first = CONV_HALO - (CONV_WIDTH - 1)
    for r0 in range(0, ts, rc):
        acc = jnp.zeros((rc, CONV_DIM), F32) + bdw_ref[...]
        for w in range(CONV_WIDTH):
            s0 = r0 + first + w
            acc = acc + wdw_ref[w:w + 1, :] * ycat_ref[s0:s0 + rc, :]
        mu = jnp.mean(acc, axis=-1, keepdims=True)
        d = acc - mu
        var = jnp.mean(d * d, axis=-1, keepdims=True)
        y = d * lax.rsqrt(var + EPS) * lng_ref[...] + lnb_ref[...]
        o_ref[0, r0:r0 + rc, :] = (y * _sigmoid(y)).astype(BF)


def _conv_branch(glu, wdw, bdw, lng, lnb):
    B, S, C = glu.shape
    ts = min(TS_CONV, S)
    hb = ts // CONV_HALO
    vec = lambda b, i: (0, 0)
    return pl.pallas_call(
        functools.partial(_conv_kernel, ts=ts),
        grid=(B, S // ts),
        in_specs=[
            pl.BlockSpec((1, CONV_HALO, C), lambda b, i: (b, jnp.maximum(i * hb - 1, 0), 0)),
            pl.BlockSpec((1, ts, C), lambda b, i: (b, i, 0)),
            pl.BlockSpec((CONV_WIDTH, C), vec),
            pl.BlockSpec((1, C), vec),
            pl.BlockSpec((1, C), vec),
            pl.BlockSpec((1, C), vec),
        ],
        out_specs=pl.BlockSpec((1, ts, C), lambda b, i: (b, i, 0)),
        out_shape=jax.ShapeDtypeStruct((B, S, C), BF),
        scratch_shapes=[pltpu.VMEM((ts + CONV_HALO, C), F32)],
        compiler_params=_cparams(("parallel", "parallel")),
        name="conv_branch",
    )(glu, glu, wdw, bdw.reshape(1, C), lng.reshape(1, C), lnb.reshape(1, C))


def _sb_kernel(q_ref, k_ref, v_ref, uo_ref, o_ref, acc_ref, carry_ref):
    i = pl.program_id(2)
    tb = SB_BLOCK
    tq = SB_TQ
    blocks = list(reversed(range(tq // tb)))
    lane = lax.broadcasted_iota(I32, (1, LANES), 1)
    head_masks = (lane < SB_HEAD_DIM, lane >= SB_HEAD_DIM)
    q = q_ref[0] * (SB_HEAD_DIM ** -0.5)
    qz = jnp.zeros_like(q)
    qh = [jnp.where(m, q, qz) for m in head_masks]
    uo = uo_ref[...]
    acc_ref[...] = jnp.zeros_like(acc_ref)
    carry_ref[...] = jnp.zeros_like(carry_ref)
    row = lax.broadcasted_iota(I32, (tb, tb), 0)
    col = lax.broadcasted_iota(I32, (tb, tb), 1)
    causal = col < row

    def mask_diag(a):
        top = jnp.where(causal, a[:tb], 0.0)
        return top if a.shape[0] == tb else jnp.concatenate([top, a[tb:]], axis=0)

    def group(base, band):
        r0s = {jb: (jb * tb if band else 0) for jb in blocks}
        zs, vvs = {}, {}
        for jb in blocks:
            start = pl.multiple_of(base + jb * tb, tb)
            k = k_ref[0, pl.ds(start, tb), :]
            v = v_ref[0, pl.ds(start, tb), :]
            vz = jnp.zeros_like(v)
            vvs[jb] = jnp.concatenate([jnp.where(m, v, vz) for m in head_masks], axis=0)
            for h in range(2):
                zs[jb, h] = _dot_nt(qh[h][r0s[jb]:], k)
        lbs, rs = {}, {}
        for key, z in zs.items():
            lb = _log_sigmoid(z)
            lk = lb - z
            if band:
                lk = mask_diag(lk)
            lbs[key] = lb
            rs[key] = _dot(lk.astype(BF), uo)
        car = [carry_ref[h] for h in range(2)]
        atts = {}
        for jb in blocks:
            r0 = r0s[jb]
            for h in range(2):
                r = rs[jb, h]
                att = jnp.exp(lbs[jb, h] + r[:, :tb] + car[h][r0:])
                if band:
                    att = mask_diag(att)
                atts[jb, h] = att.astype(BF)
                upd = car[h][r0:] + r[:, tb:]
                car[h] = upd if r0 == 0 else jnp.concatenate([car[h][:r0], upd], axis=0)
        for h in range(2):
            carry_ref[h] = car[h]
        if band:
            for jb in blocks:
                both = jnp.concatenate([atts[jb, 0], atts[jb, 1]], axis=1)
                acc_ref[r0s[jb]:, :] += _dot(both, vvs[jb])
        else:
            both = jnp.concatenate([atts[jb, h] for jb in blocks for h in range(2)], axis=1)
            acc_ref[...] += _dot(both, jnp.concatenate([vvs[jb] for jb in blocks], axis=0))

    group(pl.multiple_of(i * tq, tq), True)

    def body(g, c):
        group(pl.multiple_of((i - 1 - g) * tq, tq), False)
        return c

    lax.fori_loop(0, i, body, 0)
    o_ref[0] = acc_ref[...].astype(BF)


def _sb_consts():
    tb = SB_BLOCK
    j = jnp.arange(tb)[:, None]
    s = jnp.arange(tb)[None, :]
    return jnp.concatenate([(j > s).astype(BF), jnp.ones((tb, tb), BF)], axis=1)


def _sb_attention(sb):
    B, S, _ = sb.shape
    tb = SB_BLOCK
    tq = SB_TQ
    npair = SB_DIM // LANES
    return pl.pallas_call(
        _sb_kernel,
        grid=(B, npair, S // tq),
        in_specs=[
            pl.BlockSpec((1, tq, LANES), lambda b, p, i: (b, i, p)),
            pl.BlockSpec((1, S, LANES), lambda b, p, i: (b, 0, npair + p)),
            pl.BlockSpec((1, S, LANES), lambda b, p, i: (b, 0, 2 * npair + p)),
            pl.BlockSpec((tb, 2 * tb), lambda b, p, i: (0, 0)),
        ],
        out_specs=pl.BlockSpec((1, tq, LANES), lambda b, p, i: (b, i, p)),
        out_shape=jax.ShapeDtypeStruct((B, S, SB_DIM), BF),
        scratch_shapes=[pltpu.VMEM((tq, LANES), F32), pltpu.VMEM((2, tq, LANES), F32)],
        compiler_params=_cparams(("parallel", "parallel", "parallel")),
        name="sb_attention",
    )(sb, sb, sb, _sb_consts())


def _gla_kernel(q_ref, k_ref, v_ref, gg_ref, glr_ref, a2_ref, ab_ref, ng_ref, lc_ref, mk_ref,
                o_ref, st_ref, *, t):
    @pl.when(pl.program_id(2) == 0)
    def _():
        st_ref[...] = jnp.zeros_like(st_ref)

    cs = GLA_CHUNK
    nc = t // cs
    scale = GLA_DK ** -0.5
    u = _dot(glr_ref[0], a2_ref[...]) + ab_ref[...]
    la = _log_sigmoid(u) * (1.0 / GLA_TAU)
    hi, lo = _split_bf16(la)
    lc = lc_ref[...]
    bs = [_dot(lc, jnp.concatenate([hi[c * cs:(c + 1) * cs], lo[c * cs:(c + 1) * cs]], axis=0))
          for c in range(nc)]
    b = jnp.concatenate(bs, axis=0)
    btot = jnp.concatenate([jnp.broadcast_to(bc[cs - 1:cs], (cs, GLA_DK)) for bc in bs], axis=0)
    q = q_ref[0].astype(F32) * scale
    k = k_ref[0].astype(F32)
    v = v_ref[0]
    qd = (q * jnp.exp(b)).astype(BF)
    ki = (k * jnp.exp(-b)).astype(BF)
    ke = (k * jnp.exp(btot - b)).astype(BF)
    sc = jnp.where(mk_ref[...] > 0.0, _dot_nt(qd, ki), 0.0).astype(BF)
    o = _dot(sc, v)
    ds = [_dot_tn(v[c * cs:(c + 1) * cs], ke[c * cs:(c + 1) * cs]) for c in range(nc)]
    st = st_ref[...]
    inter = []
    for c in range(nc):
        inter.append(_dot_nt(qd[c * cs:(c + 1) * cs], st.astype(BF)))
        st = st * jnp.exp(bs[c][cs - 1:cs]) + ds[c]
    st_ref[...] = st
    o = o + jnp.concatenate(inter, axis=0)
    o_ref[0] = (_rms(o) * ng_ref[...] * gg_ref[0].astype(F32)).astype(BF)


def _gla_consts(t):
    cs = GLA_CHUNK
    r = jnp.arange(t)[:, None]
    c = jnp.arange(t)[None, :]
    mask = ((r // cs) == (c // cs)) & (c <= r)
    low = mask[:cs, :cs].astype(BF)
    return jnp.concatenate([low, low], axis=1), mask.astype(F32)


def _gla_branch(gla, a2p, ab, ng):
    B, S, _ = gla.shape
    t = min(T_GLA, S)
    H = GLA_HEADS
    col = lambda off: (lambda b, h, s: (b, s, off + h))
    par = lambda b, h, s: (0, h)
    const = lambda b, h, s: (0, 0)
    lc, mk = _gla_consts(t)
    return pl.pallas_call(
        functools.partial(_gla_kernel, t=t),
        grid=(B, H, S // t),
        in_specs=[
            pl.BlockSpec((1, t, LANES), col(0)),
            pl.BlockSpec((1, t, LANES), col(H)),
            pl.BlockSpec((1, t, LANES), col(2 * H)),
            pl.BlockSpec((1, t, LANES), col(3 * H)),
            pl.BlockSpec((1, t, LANES), lambda b, h, s: (b, s, 4 * H)),
            pl.BlockSpec((LANES, GLA_DK), par),
            pl.BlockSpec((1, GLA_DK), par),
            pl.BlockSpec((1, GLA_DV), par),
            pl.BlockSpec((GLA_CHUNK, 2 * GLA_CHUNK), const),
            pl.BlockSpec((t, t), const),
        ],
        out_specs=pl.BlockSpec((1, t, LANES), lambda b, h, s: (b, s, h)),
        out_shape=jax.ShapeDtypeStruct((B, S, GLA_DIM), BF),
        scratch_shapes=[pltpu.VMEM((GLA_DV, GLA_DK), F32)],
        compiler_params=_cparams(("parallel", "parallel", "arbitrary")),
        name="gla_branch",
    )(gla, gla, gla, gla, gla, a2p, ab.reshape(1, GLA_DIM), ng.reshape(1, GLA_DIM), lc, mk)


def _mixer_out_kernel(x_ref, ca_ref, ob_ref, oc_ref, gt_ref, wc_ref, wb_ref, wg_ref, wo_ref,
                      gm_ref, pg_ref, o_ref):
    D = x_ref.shape[-1]
    gt = gt_ref[0]
    m = gt[:, 0:D].astype(F32) * _dot(ca_ref[0], wc_ref[...])
    m = m + gt[:, D:2 * D].astype(F32) * _dot(ob_ref[0], wb_ref[...])
    m = m + gt[:, 2 * D:3 * D].astype(F32) * _dot(oc_ref[0], wg_ref[...])
    y = _dot(m.astype(BF), wo_ref[...])
    o_ref[0] = x_ref[0] + gm_ref[0] * (_rms(y) * pg_ref[...])


def _mixer_out(x, ca, ob, oc, gates, wc, wb, wg, wo, modr, l, post_g):
    B, S, D = x.shape
    tm = min(TM_OUT, S)
    tok = lambda w: pl.BlockSpec((1, tm, w), lambda b, i: (b, i, 0))
    full = lambda a: pl.BlockSpec(a.shape, lambda b, i: (0, 0))
    return pl.pallas_call(
        _mixer_out_kernel,
        grid=(B, S // tm),
        in_specs=[tok(D), tok(CONV_DIM), tok(SB_DIM), tok(GLA_DIM), tok(C_GATE),
                  full(wc), full(wb), full(wg), full(wo),
                  _mod_spec(l, 2, B, D, lambda b, i: b),
                  pl.BlockSpec((1, D), lambda b, i: (0, 0))],
        out_specs=tok(D),
        out_shape=jax.ShapeDtypeStruct((B, S, D), F32),
        compiler_params=_cparams(("parallel", "parallel")),
        name="mixer_out",
    )(x, ca, ob, oc, gates, wc, wb, wg, wo, modr, post_g.reshape(1, D))


def _ffn_kernel(x_ref, sc_ref, sh_ref, pre_ref, wg_ref, wu_ref, wd_ref, gf_ref, post_ref, o_ref,
                h_ref, acc_ref):
    j = pl.program_id(2)

    @pl.when(j == 0)
    def _():
        h_ref[...] = _prenorm(x_ref[0], pre_ref[...], sc_ref[0], sh_ref[0]).astype(BF)
        acc_ref[...] = jnp.zeros_like(acc_ref)

    h = h_ref[...]
    g = _dot(h, wg_ref[...])
    u = _dot(h, wu_ref[...])
    acc_ref[...] += _dot((g * _sigmoid(g) * u).astype(BF), wd_ref[...])

    @pl.when(j == pl.num_programs(2) - 1)
    def _():
        o_ref[0] = x_ref[0] + gf_ref[0] * (_rms(acc_ref[...]) * post_ref[...])


def _ffn_dense(x, modr, l, pre_g, post_g, wg, wu, wd):
    B, S, D = x.shape
    F = wg.shape[1]
    tm = min(TM_FFN, S)
    tf = F // 2 if (F // 2) % LANES == 0 else F
    bof = lambda b, i, j: b
    return pl.pallas_call(
        _ffn_kernel,
        grid=(B, S // tm, F // tf),
        in_specs=[
            pl.BlockSpec((1, tm, D), lambda b, i, j: (b, i, 0)),
            _mod_spec(l, 4, B, D, bof),
            _mod_spec(l, 3, B, D, bof),
            pl.BlockSpec((1, D), lambda b, i, j: (0, 0)),
            pl.BlockSpec((D, tf), lambda b, i, j: (0, j)),
            pl.BlockSpec((D, tf), lambda b, i, j: (0, j)),
            pl.BlockSpec((tf, D), lambda b, i, j: (j, 0)),
            _mod_spec(l, 5, B, D, bof),
            pl.BlockSpec((1, D), lambda b, i, j: (0, 0)),
        ],
        out_specs=pl.BlockSpec((1, tm, D), lambda b, i, j: (b, i, 0)),
        out_shape=jax.ShapeDtypeStruct((B, S, D), F32),
        scratch_shapes=[pltpu.VMEM((tm, D), BF), pltpu.VMEM((tm, D), F32)],
        compiler_params=_cparams(("parallel", "parallel", "arbitrary")),
        name="ffn_dense",
    )(x, modr, modr, pre_g.reshape(1, D), wg, wu, wd, modr, post_g.reshape(1, D))


def _router_kernel(x_ref, sc_ref, sh_ref, pre_ref, rw_ref, h_ref, comb_ref):
    h = _prenorm(x_ref[0], pre_ref[...], sc_ref[0], sh_ref[0])
    h_ref[0] = h.astype(BF)
    hh, hl = _split_bf16(h)
    rh, rl = _split_bf16(rw_ref[...])
    lg = _dot(hh, rh) + _dot(hh, rl) + _dot(hl, rh)
    lane = lax.broadcasted_iota(I32, lg.shape, 1).astype(F32)
    ninf = jnp.float32(-jnp.inf)
    lg = jnp.where(lane < N_EXPERTS, lg, ninf)
    m1 = jnp.max(lg, axis=-1, keepdims=True)
    i1 = jnp.min(jnp.where(lg == m1, lane, float(LANES)), axis=-1, keepdims=True)
    lg2 = jnp.where(lane == i1, ninf, lg)
    m2 = jnp.max(lg2, axis=-1, keepdims=True)
    i2 = jnp.min(jnp.where(lg2 == m2, lane, float(LANES)), axis=-1, keepdims=True)
    e = jnp.exp(m2 - m1)
    p1 = 1.0 / (1.0 + e)
    comb_ref[0] = jnp.where(lane == i1, p1, jnp.where(lane == i2, e * p1, -1.0))


def _router(x, modr, l, pre_g, rw_p):
    B, S, D = x.shape
    tm = min(TM_FFN, S)
    bof = lambda b, i: b
    return pl.pallas_call(
        _router_kernel,
        grid=(B, S // tm),
        in_specs=[
            pl.BlockSpec((1, tm, D), lambda b, i: (b, i, 0)),
            _mod_spec(l, 4, B, D, bof),
            _mod_spec(l, 3, B, D, bof),
            pl.BlockSpec((1, D), lambda b, i: (0, 0)),
            pl.BlockSpec((D, LANES), lambda b, i: (0, 0)),
        ],
        out_specs=[pl.BlockSpec((1, tm, D), lambda b, i: (b, i, 0)),
                   pl.BlockSpec((1, tm, LANES), lambda b, i: (b, i, 0))],
        out_shape=[jax.ShapeDtypeStruct((B, S, D), BF), jax.ShapeDtypeStruct((B, S, LANES), F32)],
        compiler_params=_cparams(("parallel", "parallel")),
        name="moe_router",
    )(x, modr, modr, pre_g.reshape(1, D), rw_p)


def _moe_plan(comb, n_tok):
    E = N_EXPERTS
    nck = n_tok // CK_MOE
    sel = comb[:, :E] >= 0.0
    seli = sel.astype(I32)
    rank_incl = jnp.cumsum(seli, axis=0)
    cnt = rank_incl[-1]
    padded = ((cnt + TM_MOE - 1) // TM_MOE) * TM_MOE
    seg_end = jnp.cumsum(padded)
    seg_start = seg_end - padded
    total = seg_end[-1]
    pos = jnp.where(sel, seg_start[None, :] + rank_incl - seli, -1)
    pos_ck = pos.reshape(nck, CK_MOE, E).transpose(0, 2, 1)
    comb_ck = jnp.where(sel, comb[:, :E], 0.0).reshape(nck, CK_MOE, E).transpose(0, 2, 1)
    cnt_ck = seli.reshape(nck, CK_MOE, E).sum(axis=1)
    cex = jnp.concatenate([jnp.zeros((1, E), I32), jnp.cumsum(cnt_ck, axis=0)], axis=0)

    rows_pad = 2 * n_tok + E * TM_MOE
    jt = jnp.arange(rows_pad // TG_MOE, dtype=I32) * TG_MOE
    te = jnp.minimum(jnp.sum(seg_end[None, :] <= jt[:, None], axis=1), E - 1).astype(I32)
    r0 = jt - seg_start[te]
    r1 = jnp.minimum(r0 + TG_MOE, cnt[te])
    valid = (jt < total) & (r1 > r0)
    cte = cex[1:, :][:, te]
    c_lo = jnp.sum(cte <= r0[None, :], axis=0).astype(I32)
    c_hi = jnp.sum(cte <= (r1 - 1)[None, :], axis=0).astype(I32)
    c_lo = jnp.where(valid, c_lo, 1)
    c_hi = jnp.where(valid, jnp.minimum(c_hi, nck - 1), 0)
    jf = jnp.arange(rows_pad // TM_MOE, dtype=I32) * TM_MOE
    fe = jnp.minimum(jnp.sum(seg_end[None, :] <= jf[:, None], axis=1), E - 1).astype(I32)
    fvalid = (jf < total).astype(I32)
    w0 = seg_start[None, :] + cex[:-1, :]
    nb = rows_pad // CK_MOE
    b0 = jnp.minimum(w0 // CK_MOE, nb - 1).astype(I32)
    b1 = jnp.minimum(b0 + 1, nb - 1).astype(I32)
    dup = (b1 == b0).astype(I32)
    return dict(pos_ck=pos_ck, comb_ck=comb_ck, te=te, c_lo=c_lo, c_hi=c_hi, fe=fe, fvalid=fvalid,
                b0=b0.reshape(-1), b1=b1.reshape(-1), dup=dup.reshape(-1), rows_pad=rows_pad)


def _gather_kernel(te_ref, clo_ref, chi_ref, h_ref, pos_ref, o_ref, acc_ref):
    j = pl.program_id(0)
    e = te_ref[j]
    rows = j * TG_MOE + lax.broadcasted_iota(I32, (TG_MOE, CK_MOE), 0)
    acc_ref[...] = jnp.zeros_like(acc_ref)

    def body(c, carry):
        p = pos_ref[c, pl.ds(e, 1), :]
        onehot = jnp.where(rows == p, 1.0, 0.0).astype(BF)
        start = pl.multiple_of(c * CK_MOE, CK_MOE)
        acc_ref[...] += _dot(onehot, h_ref[pl.ds(start, CK_MOE), :])
        return carry

    lax.fori_loop(clo_ref[j], chi_ref[j] + 1, body, 0)
    o_ref[...] = acc_ref[...].astype(BF)


def _moe_gather(h2, plan):
    n_tok, D = h2.shape
    rows_pad = plan["rows_pad"]
    grid_spec = pltpu.PrefetchScalarGridSpec(
        num_scalar_prefetch=3,
        grid=(rows_pad // TG_MOE,),
        in_specs=[pl.BlockSpec(memory_space=pltpu.VMEM), pl.BlockSpec(memory_space=pltpu.VMEM)],
        out_specs=pl.BlockSpec((TG_MOE, D), lambda j, *_: (j, 0)),
        scratch_shapes=[pltpu.VMEM((TG_MOE, D), F32)],
    )
    return pl.pallas_call(
        _gather_kernel,
        grid_spec=grid_spec,
        out_shape=jax.ShapeDtypeStruct((rows_pad, D), BF),
        compiler_params=_cparams(("arbitrary",)),
        name="moe_gather",
    )(plan["te"], plan["c_lo"], plan["c_hi"], h2, plan["pos_ck"])


def _moe_ffn_kernel(fe_ref, fv_ref, x_ref, wg_ref, wu_ref, wd_ref, o_ref, acc_ref):
    j = pl.program_id(0)
    f = pl.program_id(1)

    @pl.when(f == 0)
    def _():
        acc_ref[...] = jnp.zeros_like(acc_ref)

    @pl.when(fv_ref[j] == 1)
    def _():
        x = x_ref[...]
        g = _dot(x, wg_ref[0, 0].astype(BF))
        u = _dot(x, wu_ref[0, 0].astype(BF))
        acc_ref[...] += _dot((g * _sigmoid(g) * u).astype(BF), wd_ref[0, 0].astype(BF))

    @pl.when(f == pl.num_programs(1) - 1)
    def _():
        o_ref[...] = acc_ref[...].astype(BF)


def _moe_ffn(xs, plan, wg, wu, wd, lm):
    rows_pad, D = xs.shape
    F = wg.shape[3]
    tf = TF_MOE if F % TF_MOE == 0 else F
    nf = F // tf
    fidx = lambda f, fv, j: f * fv[j] + (nf - 1) * (1 - fv[j])
    grid_spec = pltpu.PrefetchScalarGridSpec(
        num_scalar_prefetch=2,
        grid=(rows_pad // TM_MOE, nf),
        in_specs=[
            pl.BlockSpec((TM_MOE, D), lambda j, f, fe, fv: (j, 0)),
            pl.BlockSpec((1, 1, D, tf), lambda j, f, fe, fv: (lm, fe[j], 0, fidx(f, fv, j))),
            pl.BlockSpec((1, 1, D, tf), lambda j, f, fe, fv: (lm, fe[j], 0, fidx(f, fv, j))),
            pl.BlockSpec((1, 1, tf, D), lambda j, f, fe, fv: (lm, fe[j], fidx(f, fv, j), 0)),
        ],
        out_specs=pl.BlockSpec((TM_MOE, D), lambda j, f, fe, fv: (j, 0)),
        scratch_shapes=[pltpu.VMEM((TM_MOE, D), F32)],
    )
    return pl.pallas_call(
        _moe_ffn_kernel,
        grid_spec=grid_spec,
        out_shape=jax.ShapeDtypeStruct((rows_pad, D), BF),
        compiler_params=_cparams(("arbitrary", "arbitrary")),
        name="moe_ffn",
    )(plan["fe"], plan["fvalid"], xs, wg, wu, wd)


def _combine_kernel(b0_ref, b1_ref, dup_ref, *refs):
    E = N_EXPERTS
    y_refs = refs[:2 * E]
    pos_ref, comb_ref, x_ref, gf_ref, post_ref, o_ref = refs[2 * E:]
    c = pl.program_id(0)
    r = lax.broadcasted_iota(I32, (CK_MOE, CK_MOE), 0)
    acc = jnp.zeros(o_ref.shape, F32)
    for e in range(E):
        idx = c * E + e
        p = pos_ref[0, e:e + 1, :]
        w = comb_ref[0, e:e + 1, :]
        w0 = jnp.where(p == b0_ref[idx] * CK_MOE + r, w, 0.0).astype(BF)
        keep = (1 - dup_ref[idx]).astype(F32)
        w1 = jnp.where(p == b1_ref[idx] * CK_MOE + r, w * keep, 0.0).astype(BF)
        acc = acc + _dot_tn(w0, y_refs[2 * e][...])
        acc = acc + _dot_tn(w1, y_refs[2 * e + 1][...])
    o_ref[...] = x_ref[...] + gf_ref[0] * (_rms(acc) * post_ref[...])


def _moe_combine(x2, ys, plan, modr, l, post_g, B):
    n_tok, D = x2.shape
    nck = n_tok // CK_MOE
    ck_per_b = nck // B
    E = N_EXPERTS
    base = (l * N_ADA + 5) * B
    y_specs = []
    for e in range(E):
        y_specs.append(pl.BlockSpec((CK_MOE, D), lambda c, b0, b1, du, e=e: (b0[c * E + e], 0)))
        y_specs.append(pl.BlockSpec((CK_MOE, D), lambda c, b0, b1, du, e=e: (b1[c * E + e], 0)))
    grid_spec = pltpu.PrefetchScalarGridSpec(
        num_scalar_prefetch=3,
        grid=(nck,),
        in_specs=y_specs + [
            pl.BlockSpec((1, E, CK_MOE), lambda c, *_: (c, 0, 0)),
            pl.BlockSpec((1, E, CK_MOE), lambda c, *_: (c, 0, 0)),
            pl.BlockSpec((CK_MOE, D), lambda c, *_: (c, 0)),
            pl.BlockSpec((1, 1, D), lambda c, *_: (base + c // ck_per_b, 0, 0)),
            pl.BlockSpec((1, D), lambda c, *_: (0, 0)),
        ],
        out_specs=pl.BlockSpec((CK_MOE, D), lambda c, *_: (c, 0)),
    )
    return pl.pallas_call(
        _combine_kernel,
        grid_spec=grid_spec,
        out_shape=jax.ShapeDtypeStruct((n_tok, D), F32),
        compiler_params=_cparams(("arbitrary",)),
        name="moe_combine",
    )(plan["b0"], plan["b1"], plan["dup"], *([ys] * (2 * E)), plan["pos_ck"], plan["comb_ck"],
      x2, modr, post_g.reshape(1, D))


def _moe_layer(x, modr, l, pre_g, post_g, router_w, wg, wu, wd, lm):
    B, S, D = x.shape
    n_tok = B * S
    rw_p = jnp.zeros((D, LANES), F32).at[:, :N_EXPERTS].set(router_w)
    h, comb = _router(x, modr, l, pre_g, rw_p)
    plan = _moe_plan(comb.reshape(n_tok, LANES), n_tok)
    xs = _moe_gather(h.reshape(n_tok, D), plan)
    ys = _moe_ffn(xs, plan, wg, wu, wd, lm)
    out = _moe_combine(x.reshape(n_tok, D), ys, plan, modr, l, post_g, B)
    return out.reshape(B, S, D)


def _pad_w_in(w_in_l):
    split = C_CONV + C_SB + 4 * GLA_DIM + GLA_RANK
    D = w_in_l.shape[0]
    pad = jnp.zeros((D, LANES - GLA_RANK), w_in_l.dtype)
    return jnp.concatenate([w_in_l[:, :split], pad, w_in_l[:, split:]], axis=1).astype(BF)


def kernel(x, c, ada_w, ada_b, mix_pre_g, mix_post_g, ffn_pre_g, ffn_post_g, w_in, gla_a2, gla_a_b, conv_dw, conv_dw_b, conv_ln_g, conv_ln_b, gla_norm_g, w_conv_out, w_sb_out, w_gla_out, w_o, ffn_w_gate, ffn_w_up, ffn_w_down, router_w, moe_w_gate, moe_w_up, moe_w_down):
    depth = w_in.shape[0]
    modr = _ada_mod(c, ada_w, ada_b)
    for l in range(depth):
        glu, sb, gla, gates = _mixer_in(x, modr, l, mix_pre_g[l], _pad_w_in(w_in[l]))
        ca = _conv_branch(glu, conv_dw[l], conv_dw_b[l], conv_ln_g[l], conv_ln_b[l])
        ob = _sb_attention(sb)
        a2p = jnp.zeros((LANES, GLA_DIM), F32).at[:GLA_RANK].set(gla_a2[l]).astype(BF)
        oc = _gla_branch(gla, a2p, gla_a_b[l], gla_norm_g[l])
        x = _mixer_out(x, ca, ob, oc, gates, w_conv_out[l].astype(BF), w_sb_out[l].astype(BF),
                       w_gla_out[l].astype(BF), w_o[l].astype(BF), modr, l, mix_post_g[l])
        j = l // 2
        if l % 2 == 0:
            x = _ffn_dense(x, modr, l, ffn_pre_g[l], ffn_post_g[l], ffn_w_gate[j].astype(BF),
                           ffn_w_up[j].astype(BF), ffn_w_down[j].astype(BF))
        else:
            x = _moe_layer(x, modr, l, ffn_pre_g[l], ffn_post_g[l], router_w[j],
                           moe_w_gate, moe_w_up, moe_w_down, j)
    return x
```

```python
import functools

import jax
import jax.numpy as jnp
from jax import lax
from jax.experimental import pallas as pl
from jax.experimental.pallas import tpu as pltpu

BF = jnp.bfloat16
F32 = jnp.float32
I32 = jnp.int32

EPS = 1e-6
LOG2E = 1.4426950408889634
N_ADA = 6
CONV_DIM = 512
CONV_WIDTH = 31
CONV_HALO = 32
SB_HEADS = 8
SB_HEAD_DIM = 64
SB_DIM = SB_HEADS * SB_HEAD_DIM
SB_QSCALE = SB_HEAD_DIM ** -0.5 * LOG2E
SB_BLOCK = 128
SB_TQ = 512
GLA_HEADS = 4
GLA_DK = 128
GLA_DV = 128
GLA_DIM = GLA_HEADS * GLA_DK
GLA_RANK = 16
GLA_TAU = 16.0
GLA_CHUNK = 64
N_EXPERTS = 8
LANES = 128
SUBLANES = 8
VMEM_LIMIT = 56 * 1024 * 1024

TM_IN = 256
TS_CONV = 512
T_GLA = 512
TM_OUT = 512
TM_FFN = 512
TM_MOE = 1024
TF_MOE = 512
TG_MOE = 256
CK_MOE = 256


def _cparams(sem):
    return pltpu.CompilerParams(dimension_semantics=sem, vmem_limit_bytes=VMEM_LIMIT)


def _dot(a, b):
    return jnp.dot(a, b, preferred_element_type=F32)


def _dot_nt(a, b):
    return lax.dot_general(a, b, (((1,), (1,)), ((), ())), preferred_element_type=F32)


def _dot_tn(a, b):
    return lax.dot_general(a, b, (((0,), (0,)), ((), ())), preferred_element_type=F32)


def _sigmoid(x):
    return 1.0 / (1.0 + jnp.exp(-x))


def _log_sigmoid(x):
    return jnp.minimum(x, 0.0) - jnp.log(1.0 + jnp.exp2(jnp.abs(x) * (-LOG2E)))


def _rms(y):
    return y * lax.rsqrt(jnp.mean(y * y, axis=-1, keepdims=True) + EPS)


def _split_bf16(x):
    hi = x.astype(BF)
    lo = (x - hi.astype(F32)).astype(BF)
    return hi, lo


def _ada_kernel(c_ref, w_ref, b_ref, o_ref):
    c = c_ref[...]
    rows = c.shape[0]
    ch, cl = _split_bf16(c * _sigmoid(c))
    wh, wl = _split_bf16(w_ref[0])
    top = _dot(jnp.concatenate([ch, cl], axis=0), wh)
    o_ref[0] = top[:rows] + top[rows:] + _dot(ch, wl) + b_ref[0]


def _ada_mod(c, ada_w, ada_b):
    L, D, D6 = ada_w.shape
    B = c.shape[0]
    rows = 16
    cp = jnp.zeros((rows, D), F32).at[:B].set(c)
    tn = 1536
    out = pl.pallas_call(
        _ada_kernel,
        grid=(L, D6 // tn),
        in_specs=[
            pl.BlockSpec((rows, D), lambda l, j: (0, 0)),
            pl.BlockSpec((1, D, tn), lambda l, j: (l, 0, j)),
            pl.BlockSpec((1, 1, tn), lambda l, j: (l, 0, j)),
        ],
        out_specs=pl.BlockSpec((1, rows, tn), lambda l, j: (l, 0, j)),
        out_shape=jax.ShapeDtypeStruct((L, rows, D6), F32),
        compiler_params=_cparams(("parallel", "parallel")),
        name="ada_mod",
    )(cp, ada_w, ada_b.reshape(L, 1, D6))
    mod = out[:, :B].reshape(L, B, N_ADA, D).transpose(0, 2, 1, 3)
    return mod.reshape(L * N_ADA * B, 1, D)


def _mod_spec(l, k, B, D, batch_of):
    base = (l * N_ADA + k) * B
    return pl.BlockSpec((1, 1, D), lambda *g: (base + batch_of(*g), 0, 0))


def _prenorm(x, g, sc, sh):
    return _rms(x) * g * (1.0 + sc) + sh


C_CONV = 2 * CONV_DIM
C_SB = 3 * SB_DIM
C_GLA = 4 * GLA_DIM + LANES
C_GATE = 3 * 1024


def _mixer_in_kernel(x_ref, sc_ref, sh_ref, g_ref, w_ref, glu_ref, sb_ref, gla_ref, gate_ref):
    h = _prenorm(x_ref[0], g_ref[...], sc_ref[0], sh_ref[0]).astype(BF)
    o0 = 0
    u = _dot(h, w_ref[:, o0:o0 + C_CONV])
    glu_ref[0] = (u[:, :CONV_DIM] * _sigmoid(u[:, CONV_DIM:])).astype(BF)
    o0 += C_CONV
    sb_ref[0, :, 0:SB_DIM] = (_dot(h, w_ref[:, o0:o0 + SB_DIM]) * SB_QSCALE).astype(BF)
    sb_ref[0, :, SB_DIM:] = _dot(h, w_ref[:, o0 + SB_DIM:o0 + C_SB]).astype(BF)
    o0 += C_SB
    nq = 3 * GLA_DIM
    gla_ref[0, :, 0:nq] = _dot(h, w_ref[:, o0:o0 + nq]).astype(BF)
    gg = _dot(h, w_ref[:, o0 + nq:o0 + nq + GLA_DIM])
    gla_ref[0, :, nq:nq + GLA_DIM] = (gg * _sigmoid(gg)).astype(BF)
    gla_ref[0, :, nq + GLA_DIM:] = _dot(h, w_ref[:, o0 + nq + GLA_DIM:o0 + C_GLA]).astype(BF)
    o0 += C_GLA
    gate_ref[0] = _sigmoid(_dot(h, w_ref[:, o0:o0 + C_GATE])).astype(BF)


def _mixer_in(x, modr, l, pre_g, w_in_p):
    B, S, D = x.shape
    tm = min(TM_IN, S)
    NC = w_in_p.shape[1]
    bof = lambda b, i: b
    outs = pl.pallas_call(
        _mixer_in_kernel,
        grid=(B, S // tm),
        in_specs=[
            pl.BlockSpec((1, tm, D), lambda b, i: (b, i, 0)),
            _mod_spec(l, 1, B, D, bof),
            _mod_spec(l, 0, B, D, bof),
            pl.BlockSpec((1, D), lambda b, i: (0, 0)),
            pl.BlockSpec((D, NC), lambda b, i: (0, 0)),
        ],
        out_specs=[
            pl.BlockSpec((1, tm, CONV_DIM), lambda b, i: (b, i, 0)),
            pl.BlockSpec((1, tm, C_SB), lambda b, i: (b, i, 0)),
            pl.BlockSpec((1, tm, C_GLA), lambda b, i: (b, i, 0)),
            pl.BlockSpec((1, tm, C_GATE), lambda b, i: (b, i, 0)),
        ],
        out_shape=[
            jax.ShapeDtypeStruct((B, S, CONV_DIM), BF),
            jax.ShapeDtypeStruct((B, S, C_SB), BF),
            jax.ShapeDtypeStruct((B, S, C_GLA), BF),
            jax.ShapeDtypeStruct((B, S, C_GATE), BF),
        ],
        compiler_params=_cparams(("parallel", "parallel")),
        name="mixer_in",
    )(x, modr, modr, pre_g.reshape(1, D), w_in_p)
    return outs


def _conv_kernel(prev_ref, cur_ref, wdw_ref, bdw_ref, lng_ref, lnb_ref, o_ref, ycat_ref, ysh_ref,
                 *, ts):
    i = pl.program_id(1)
    prev = prev_ref[0].astype(F32)
    ycat_ref[0:CONV_HALO, :] = jnp.where(i > 0, prev, 0.0)
    ycat_ref[CONV_HALO:, :] = cur_ref[0].astype(F32)
    nsh = ysh_ref.shape[1]
    for s in range(1, SUBLANES):
        ysh_ref[s - 1] = ycat_ref[s:s + nsh, :]
    rc = 64
    first = CONV_HALO - (CONV_WIDTH - 1)
    for r0 in range(0, ts, rc):
        acc = jnp.zeros((rc, CONV_DIM), F32) + bdw_ref[...]
        for w in range(CONV_WIDTH):
            s = (first + w) % SUBLANES
            a0 = r0 + (first + w) - s
            win = ycat_ref[a0:a0 + rc, :] if s == 0 else ysh_ref[s - 1, a0:a0 + rc, :]
            acc = acc + wdw_ref[w:w + 1, :] * win
        mu = jnp.mean(acc, axis=-1, keepdims=True)
        d = acc - mu
        var = jnp.mean(d * d, axis=-1, keepdims=True)
        y = d * lax.rsqrt(var + EPS) * lng_ref[...] + lnb_ref[...]
        o_ref[0, r0:r0 + rc, :] = (y * _sigmoid(y)).astype(BF)


def _conv_branch(glu, wdw, bdw, lng, lnb):
    B, S, C = glu.shape
    ts = min(TS_CONV, S)
    hb = ts // CONV_HALO
    vec = lambda b, i: (0, 0)
    return pl.pallas_call(
        functools.partial(_conv_kernel, ts=ts),
        grid=(B, S // ts),
        in_specs=[
            pl.BlockSpec((1, CONV_HALO, C), lambda b, i: (b, jnp.maximum(i * hb - 1, 0), 0)),
            pl.BlockSpec((1, ts, C), lambda b, i: (b, i, 0)),
            pl.BlockSpec((CONV_WIDTH, C), vec),
            pl.BlockSpec((1, C), vec),
            pl.BlockSpec((1, C), vec),
            pl.BlockSpec((1, C), vec),
        ],
        out_specs=pl.BlockSpec((1, ts, C), lambda b, i: (b, i, 0)),
        out_shape=jax.ShapeDtypeStruct((B, S, C), BF),
        scratch_shapes=[pltpu.VMEM((ts + CONV_HALO, C), F32),
                        pltpu.VMEM((SUBLANES - 1, ts + CONV_HALO - SUBLANES, C), F32)],
        compiler_params=_cparams(("parallel", "parallel")),
        name="conv_branch",
    )(glu, glu, wdw, bdw.reshape(1, C), lng.reshape(1, C), lnb.reshape(1, C))


def _sb_kernel(q_ref, k_ref, v_ref, uo_ref, o_ref, acc_ref, carry_ref):
    i = pl.program_id(2)
    tb = SB_BLOCK
    tq = SB_TQ
    blocks = list(reversed(range(tq // tb)))
    lane = lax.broadcasted_iota(I32, (1, LANES), 1)
    head_masks = (lane < SB_HEAD_DIM, lane >= SB_HEAD_DIM)
    q = q_ref[0]
    qz = jnp.zeros_like(q)
    qh = [jnp.where(m, q, qz) for m in head_masks]
    uo = uo_ref[...]
    acc_ref[...] = jnp.zeros_like(acc_ref)
    carry_ref[...] = jnp.zeros_like(carry_ref)
    row = lax.broadcasted_iota(I32, (tb, tb), 0)
    col = lax.broadcasted_iota(I32, (tb, tb), 1)
    causal = col < row

    def mask_diag(a):
        top = jnp.where(causal, a[:tb], 0.0)
        return top if a.shape[0] == tb else jnp.concatenate([top, a[tb:]], axis=0)

    def group(base, band):
        r0s = {jb: (jb * tb if band else 0) for jb in blocks}
        zs, vvs = {}, {}
        for jb in blocks:
            start = pl.multiple_of(base + jb * tb, tb)
            k = k_ref[0, pl.ds(start, tb), :]
            v = v_ref[0, pl.ds(start, tb), :]
            vz = jnp.zeros_like(v)
            vvs[jb] = jnp.concatenate([jnp.where(m, v, vz) for m in head_masks], axis=0)
            for h in range(2):
                zs[jb, h] = _dot_nt(qh[h][r0s[jb]:], k)
        lbs, rs = {}, {}
        for key, z in zs.items():
            nabs = pltpu.bitcast(pltpu.bitcast(z, I32) | jnp.int32(-2 ** 31), F32)
            lb = jnp.minimum(z, 0.0) - jnp.log2(1.0 + jnp.exp2(nabs))
            lk = lb - z
            if band:
                lk = mask_diag(lk)
            lbs[key] = lb
            rs[key] = _dot(lk.astype(BF), uo)
        car = [carry_ref[h] for h in range(2)]
        atts = {}
        for jb in blocks:
            r0 = r0s[jb]
            for h in range(2):
                r = rs[jb, h]
                att = jnp.exp2(lbs[jb, h] + r[:, :tb] + car[h][r0:])
                if band:
                    att = mask_diag(att)
                atts[jb, h] = att.astype(BF)
                upd = car[h][r0:] + r[:, tb:]
                car[h] = upd if r0 == 0 else jnp.concatenate([car[h][:r0], upd], axis=0)
        for h in range(2):
            carry_ref[h] = car[h]
        if band:
            for jb in blocks:
                both = jnp.concatenate([atts[jb, 0], atts[jb, 1]], axis=1)
                acc_ref[r0s[jb]:, :] += _dot(both, vvs[jb])
        else:
            both = jnp.concatenate([atts[jb, h] for jb in blocks for h in range(2)], axis=1)
            acc_ref[...] += _dot(both, jnp.concatenate([vvs[jb] for jb in blocks], axis=0))

    group(pl.multiple_of(i * tq, tq), True)

    def body(g, c):
        group(pl.multiple_of((i - 1 - g) * tq, tq), False)
        return c

    lax.fori_loop(0, i, body, 0)
    o_ref[0] = acc_ref[...].astype(BF)


def _sb_consts():
    tb = SB_BLOCK
    j = jnp.arange(tb)[:, None]
    s = jnp.arange(tb)[None, :]
    return jnp.concatenate([(j > s).astype(BF), jnp.ones((tb, tb), BF)], axis=1)


def _sb_attention(sb):
    B, S, _ = sb.shape
    tb = SB_BLOCK
    tq = SB_TQ
    npair = SB_DIM // LANES
    return pl.pallas_call(
        _sb_kernel,
        grid=(B, npair, S // tq),
        in_specs=[
            pl.BlockSpec((1, tq, LANES), lambda b, p, i: (b, i, p)),
            pl.BlockSpec((1, S, LANES), lambda b, p, i: (b, 0, npair + p)),
            pl.BlockSpec((1, S, LANES), lambda b, p, i: (b, 0, 2 * npair + p)),
            pl.BlockSpec((tb, 2 * tb), lambda b, p, i: (0, 0)),
        ],
        out_specs=pl.BlockSpec((1, tq, LANES), lambda b, p, i: (b, i, p)),
        out_shape=jax.ShapeDtypeStruct((B, S, SB_DIM), BF),
        scratch_shapes=[pltpu.VMEM((tq, LANES), F32), pltpu.VMEM((2, tq, LANES), F32)],
        compiler_params=_cparams(("parallel", "parallel", "parallel")),
        name="sb_attention",
    )(sb, sb, sb, _sb_consts())


def _gla_kernel(q_ref, k_ref, v_ref, gg_ref, glr_ref, a2_ref, ab_ref, ng_ref, lc_ref, mk_ref,
                o_ref, st_ref, *, t):
    @pl.when(pl.program_id(2) == 0)
    def _():
        st_ref[...] = jnp.zeros_like(st_ref)

    cs = GLA_CHUNK
    nc = t // cs
    scale = GLA_DK ** -0.5
    u = _dot(glr_ref[0], a2_ref[...]) + ab_ref[...]
    la = _log_sigmoid(u) * (1.0 / GLA_TAU)
    hi, lo = _split_bf16(la)
    lc = lc_ref[...]
    bs = [_dot(lc, jnp.concatenate([hi[c * cs:(c + 1) * cs], lo[c * cs:(c + 1) * cs]], axis=0))
          for c in range(nc)]
    b = jnp.concatenate(bs, axis=0)
    btot = jnp.concatenate([jnp.broadcast_to(bc[cs - 1:cs], (cs, GLA_DK)) for bc in bs], axis=0)
    q = q_ref[0].astype(F32) * scale
    k = k_ref[0].astype(F32)
    v = v_ref[0]
    qd = (q * jnp.exp(b)).astype(BF)
    ki = (k * jnp.exp(-b)).astype(BF)
    ke = (k * jnp.exp(btot - b)).astype(BF)
    sc = jnp.where(mk_ref[...] > 0.0, _dot_nt(qd, ki), 0.0).astype(BF)
    o = _dot(sc, v)
    ds = [_dot_tn(v[c * cs:(c + 1) * cs], ke[c * cs:(c + 1) * cs]) for c in range(nc)]
    st = st_ref[...]
    inter = []
    for c in range(nc):
        inter.append(_dot_nt(qd[c * cs:(c + 1) * cs], st.astype(BF)))
        st = st * jnp.exp(bs[c][cs - 1:cs]) + ds[c]
    st_ref[...] = st
    o = o + jnp.concatenate(inter, axis=0)
    o_ref[0] = (_rms(o) * ng_ref[...] * gg_ref[0].astype(F32)).astype(BF)


def _gla_consts(t):
    cs = GLA_CHUNK
    r = jnp.arange(t)[:, None]
    c = jnp.arange(t)[None, :]
    mask = ((r // cs) == (c // cs)) & (c <= r)
    low = mask[:cs, :cs].astype(BF)
    return jnp.concatenate([low, low], axis=1), mask.astype(F32)


def _gla_branch(gla, a2p, ab, ng):
    B, S, _ = gla.shape
    t = min(T_GLA, S)
    H = GLA_HEADS
    col = lambda off: (lambda b, h, s: (b, s, off + h))
    par = lambda b, h, s: (0, h)
    const = lambda b, h, s: (0, 0)
    lc, mk = _gla_consts(t)
    return pl.pallas_call(
        functools.partial(_gla_kernel, t=t),
        grid=(B, H, S // t),
        in_specs=[
            pl.BlockSpec((1, t, LANES), col(0)),
            pl.BlockSpec((1, t, LANES), col(H)),
            pl.BlockSpec((1, t, LANES), col(2 * H)),
            pl.BlockSpec((1, t, LANES), col(3 * H)),
            pl.BlockSpec((1, t, LANES), lambda b, h, s: (b, s, 4 * H)),
            pl.BlockSpec((LANES, GLA_DK), par),
            pl.BlockSpec((1, GLA_DK), par),
            pl.BlockSpec((1, GLA_DV), par),
            pl.BlockSpec((GLA_CHUNK, 2 * GLA_CHUNK), const),
            pl.BlockSpec((t, t), const),
        ],
        out_specs=pl.BlockSpec((1, t, LANES), lambda b, h, s: (b, s, h)),
        out_shape=jax.ShapeDtypeStruct((B, S, GLA_DIM), BF),
        scratch_shapes=[pltpu.VMEM((GLA_DV, GLA_DK), F32)],
        compiler_params=_cparams(("parallel", "parallel", "arbitrary")),
        name="gla_branch",
    )(gla, gla, gla, gla, gla, a2p, ab.reshape(1, GLA_DIM), ng.reshape(1, GLA_DIM), lc, mk)


def _mixer_out_kernel(x_ref, ca_ref, ob_ref, oc_ref, gt_ref, wc_ref, wb_ref, wg_ref, wo_ref,
                      gm_ref, pg_ref, o_ref):
    D = x_ref.shape[-1]
    gt = gt_ref[0]
    m = gt[:, 0:D].astype(F32) * _dot(ca_ref[0], wc_ref[...])
    m = m + gt[:, D:2 * D].astype(F32) * _dot(ob_ref[0], wb_ref[...])
    m = m + gt[:, 2 * D:3 * D].astype(F32) * _dot(oc_ref[0], wg_ref[...])
    y = _dot(m.astype(BF), wo_ref[...])
    o_ref[0] = x_ref[0] + gm_ref[0] * (_rms(y) * pg_ref[...])


def _mixer_out(x, ca, ob, oc, gates, wc, wb, wg, wo, modr, l, post_g):
    B, S, D = x.shape
    tm = min(TM_OUT, S)
    tok = lambda w: pl.BlockSpec((1, tm, w), lambda b, i: (b, i, 0))
    full = lambda a: pl.BlockSpec(a.shape, lambda b, i: (0, 0))
    return pl.pallas_call(
        _mixer_out_kernel,
        grid=(B, S // tm),
        in_specs=[tok(D), tok(CONV_DIM), tok(SB_DIM), tok(GLA_DIM), tok(C_GATE),
                  full(wc), full(wb), full(wg), full(wo),
                  _mod_spec(l, 2, B, D, lambda b, i: b),
                  pl.BlockSpec((1, D), lambda b, i: (0, 0))],
        out_specs=tok(D),
        out_shape=jax.ShapeDtypeStruct((B, S, D), F32),
        compiler_params=_cparams(("parallel", "parallel")),
        name="mixer_out",
    )(x, ca, ob, oc, gates, wc, wb, wg, wo, modr, post_g.reshape(1, D))


def _ffn_kernel(x_ref, sc_ref, sh_ref, pre_ref, wg_ref, wu_ref, wd_ref, gf_ref, post_ref, o_ref,
                h_ref, acc_ref):
    j = pl.program_id(2)

    @pl.when(j == 0)
    def _():
        h_ref[...] = _prenorm(x_ref[0], pre_ref[...], sc_ref[0], sh_ref[0]).astype(BF)
        acc_ref[...] = jnp.zeros_like(acc_ref)

    h = h_ref[...]
    g = _dot(h, wg_ref[...])
    u = _dot(h, wu_ref[...])
    acc_ref[...] += _dot((g * _sigmoid(g) * u).astype(BF), wd_ref[...])

    @pl.when(j == pl.num_programs(2) - 1)
    def _():
        o_ref[0] = x_ref[0] + gf_ref[0] * (_rms(acc_ref[...]) * post_ref[...])


def _ffn_dense(x, modr, l, pre_g, post_g, wg, wu, wd):
    B, S, D = x.shape
    F = wg.shape[1]
    tm = min(TM_FFN, S)
    tf = F // 2 if (F // 2) % LANES == 0 else F
    bof = lambda b, i, j: b
    return pl.pallas_call(
        _ffn_kernel,
        grid=(B, S // tm, F // tf),
        in_specs=[
            pl.BlockSpec((1, tm, D), lambda b, i, j: (b, i, 0)),
            _mod_spec(l, 4, B, D, bof),
            _mod_spec(l, 3, B, D, bof),
            pl.BlockSpec((1, D), lambda b, i, j: (0, 0)),
            pl.BlockSpec((D, tf), lambda b, i, j: (0, j)),
            pl.BlockSpec((D, tf), lambda b, i, j: (0, j)),
            pl.BlockSpec((tf, D), lambda b, i, j: (j, 0)),
            _mod_spec(l, 5, B, D, bof),
            pl.BlockSpec((1, D), lambda b, i, j: (0, 0)),
        ],
        out_specs=pl.BlockSpec((1, tm, D), lambda b, i, j: (b, i, 0)),
        out_shape=jax.ShapeDtypeStruct((B, S, D), F32),
        scratch_shapes=[pltpu.VMEM((tm, D), BF), pltpu.VMEM((tm, D), F32)],
        compiler_params=_cparams(("parallel", "parallel", "arbitrary")),
        name="ffn_dense",
    )(x, modr, modr, pre_g.reshape(1, D), wg, wu, wd, modr, post_g.reshape(1, D))


def _router_kernel(x_ref, sc_ref, sh_ref, pre_ref, rw_ref, h_ref, comb_ref):
    h = _prenorm(x_ref[0], pre_ref[...], sc_ref[0], sh_ref[0])
    h_ref[0] = h.astype(BF)
    hh, hl = _split_bf16(h)
    rh, rl = _split_bf16(rw_ref[...])
    lg = _dot(hh, rh) + _dot(hh, rl) + _dot(hl, rh)
    lane = lax.broadcasted_iota(I32, lg.shape, 1).astype(F32)
    ninf = jnp.float32(-jnp.inf)
    lg = jnp.where(lane < N_EXPERTS, lg, ninf)
    m1 = jnp.max(lg, axis=-1, keepdims=True)
    i1 = jnp.min(jnp.where(lg == m1, lane, float(LANES)), axis=-1, keepdims=True)
    lg2 = jnp.where(lane == i1, ninf, lg)
    m2 = jnp.max(lg2, axis=-1, keepdims=True)
    i2 = jnp.min(jnp.where(lg2 == m2, lane, float(LANES)), axis=-1, keepdims=True)
    e = jnp.exp(m2 - m1)
    p1 = 1.0 / (1.0 + e)
    comb_ref[0] = jnp.where(lane == i1, p1, jnp.where(lane == i2, e * p1, -1.0))


def _router(x, modr, l, pre_g, rw_p):
    B, S, D = x.shape
    tm = min(TM_FFN, S)
    bof = lambda b, i: b
    return pl.pallas_call(
        _router_kernel,
        grid=(B, S // tm),
        in_specs=[
            pl.BlockSpec((1, tm, D), lambda b, i: (b, i, 0)),
            _mod_spec(l, 4, B, D, bof),
            _mod_spec(l, 3, B, D, bof),
            pl.BlockSpec((1, D), lambda b, i: (0, 0)),
            pl.BlockSpec((D, LANES), lambda b, i: (0, 0)),
        ],
        out_specs=[pl.BlockSpec((1, tm, D), lambda b, i: (b, i, 0)),
                   pl.BlockSpec((1, tm, LANES), lambda b, i: (b, i, 0))],
        out_shape=[jax.ShapeDtypeStruct((B, S, D), BF), jax.ShapeDtypeStruct((B, S, LANES), F32)],
        compiler_params=_cparams(("parallel", "parallel")),
        name="moe_router",
    )(x, modr, modr, pre_g.reshape(1, D), rw_p)


def _moe_plan(comb, n_tok):
    E = N_EXPERTS
    nck = n_tok // CK_MOE
    sel = comb[:, :E] >= 0.0
    seli = sel.astype(I32)
    rank_incl = jnp.cumsum(seli, axis=0)
    cnt = rank_incl[-1]
    padded = ((cnt + TM_MOE - 1) // TM_MOE) * TM_MOE
    seg_end = jnp.cumsum(padded)
    seg_start = seg_end - padded
    total = seg_end[-1]
    pos = jnp.where(sel, seg_start[None, :] + rank_incl - seli, -1)
    pos_ck = pos.reshape(nck, CK_MOE, E).transpose(0, 2, 1)
    comb_ck = jnp.where(sel, comb[:, :E], 0.0).reshape(nck, CK_MOE, E).transpose(0, 2, 1)
    cnt_ck = seli.reshape(nck, CK_MOE, E).sum(axis=1)
    cex = jnp.concatenate([jnp.zeros((1, E), I32), jnp.cumsum(cnt_ck, axis=0)], axis=0)

    rows_pad = 2 * n_tok + E * TM_MOE
    jt = jnp.arange(rows_pad // TG_MOE, dtype=I32) * TG_MOE
    te = jnp.minimum(jnp.sum(seg_end[None, :] <= jt[:, None], axis=1), E - 1).astype(I32)
    r0 = jt - seg_start[te]
    r1 = jnp.minimum(r0 + TG_MOE, cnt[te])
    valid = (jt < total) & (r1 > r0)
    cte = cex[1:, :][:, te]
    c_lo = jnp.sum(cte <= r0[None, :], axis=0).astype(I32)
    c_hi = jnp.sum(cte <= (r1 - 1)[None, :], axis=0).astype(I32)
    c_lo = jnp.where(valid, c_lo, 1)
    c_hi = jnp.where(valid, jnp.minimum(c_hi, nck - 1), 0)
    jf = jnp.arange(rows_pad // TM_MOE, dtype=I32) * TM_MOE
    fe = jnp.minimum(jnp.sum(seg_end[None, :] <= jf[:, None], axis=1), E - 1).astype(I32)
    fvalid = (jf < total).astype(I32)
    w0 = seg_start[None, :] + cex[:-1, :]
    nb = rows_pad // CK_MOE
    b0 = jnp.minimum(w0 // CK_MOE, nb - 1).astype(I32)
    b1 = jnp.minimum(b0 + 1, nb - 1).astype(I32)
    dup = (b1 == b0).astype(I32)
    return dict(pos_ck=pos_ck, comb_ck=comb_ck, te=te, c_lo=c_lo, c_hi=c_hi, fe=fe, fvalid=fvalid,
                b0=b0.reshape(-1), b1=b1.reshape(-1), dup=dup.reshape(-1), rows_pad=rows_pad)


def _gather_kernel(te_ref, clo_ref, chi_ref, h_ref, pos_ref, o_ref, acc_ref):
    j = pl.program_id(0)
    e = te_ref[j]
    rows = j * TG_MOE + lax.broadcasted_iota(I32, (TG_MOE, CK_MOE), 0)
    acc_ref[...] = jnp.zeros_like(acc_ref)

    def body(c, carry):
        p = pos_ref[c, pl.ds(e, 1), :]
        onehot = jnp.where(rows == p, 1.0, 0.0).astype(BF)
        start = pl.multiple_of(c * CK_MOE, CK_MOE)
        acc_ref[...] += _dot(onehot, h_ref[pl.ds(start, CK_MOE), :])
        return carry

    lax.fori_loop(clo_ref[j], chi_ref[j] + 1, body, 0)
    o_ref[...] = acc_ref[...].astype(BF)


def _moe_gather(h2, plan):
    n_tok, D = h2.shape
    rows_pad = plan["rows_pad"]
    grid_spec = pltpu.PrefetchScalarGridSpec(
        num_scalar_prefetch=3,
        grid=(rows_pad // TG_MOE,),
        in_specs=[pl.BlockSpec(memory_space=pltpu.VMEM), pl.BlockSpec(memory_space=pltpu.VMEM)],
        out_specs=pl.BlockSpec((TG_MOE, D), lambda j, *_: (j, 0)),
        scratch_shapes=[pltpu.VMEM((TG_MOE, D), F32)],
    )
    return pl.pallas_call(
        _gather_kernel,
        grid_spec=grid_spec,
        out_shape=jax.ShapeDtypeStruct((rows_pad, D), BF),
        compiler_params=_cparams(("arbitrary",)),
        name="moe_gather",
    )(plan["te"], plan["c_lo"], plan["c_hi"], h2, plan["pos_ck"])


def _moe_ffn_kernel(fe_ref, fv_ref, x_ref, wg_ref, wu_ref, wd_ref, o_ref, acc_ref):
    j = pl.program_id(0)
    f = pl.program_id(1)

    @pl.when(f == 0)
    def _():
        acc_ref[...] = jnp.zeros_like(acc_ref)

    @pl.when(fv_ref[j] == 1)
    def _():
        x = x_ref[...]
        g = _dot(x, wg_ref[0, 0].astype(BF))
        u = _dot(x, wu_ref[0, 0].astype(BF))
        acc_ref[...] += _dot((g * _sigmoid(g) * u).astype(BF), wd_ref[0, 0].astype(BF))

    @pl.when(f == pl.num_programs(1) - 1)
    def _():
        o_ref[...] = acc_ref[...].astype(BF)


def _moe_ffn(xs, plan, wg, wu, wd, lm):
    rows_pad, D = xs.shape
    F = wg.shape[3]
    tf = TF_MOE if F % TF_MOE == 0 else F
    nf = F // tf
    fidx = lambda f, fv, j: f * fv[j] + (nf - 1) * (1 - fv[j])
    grid_spec = pltpu.PrefetchScalarGridSpec(
        num_scalar_prefetch=2,
        grid=(rows_pad // TM_MOE, nf),
        in_specs=[
            pl.BlockSpec((TM_MOE, D), lambda j, f, fe, fv: (j, 0)),
            pl.BlockSpec((1, 1, D, tf), lambda j, f, fe, fv: (lm, fe[j], 0, fidx(f, fv, j))),
            pl.BlockSpec((1, 1, D, tf), lambda j, f, fe, fv: (lm, fe[j], 0, fidx(f, fv, j))),
            pl.BlockSpec((1, 1, tf, D), lambda j, f, fe, fv: (lm, fe[j], fidx(f, fv, j), 0)),
        ],
        out_specs=pl.BlockSpec((TM_MOE, D), lambda j, f, fe, fv: (j, 0)),
        scratch_shapes=[pltpu.VMEM((TM_MOE, D), F32)],
    )
    return pl.pallas_call(
        _moe_ffn_kernel,
        grid_spec=grid_spec,
        out_shape=jax.ShapeDtypeStruct((rows_pad, D), BF),
        compiler_params=_cparams(("arbitrary", "arbitrary")),
        name="moe_ffn",
    )(plan["fe"], plan["fvalid"], xs, wg, wu, wd)


def _combine_kernel(b0_ref, b1_ref, dup_ref, *refs):
    E = N_EXPERTS
    y_refs = refs[:2 * E]
    pos_ref, comb_ref, x_ref, gf_ref, post_ref, o_ref = refs[2 * E:]
    c = pl.program_id(0)
    r = lax.broadcasted_iota(I32, (CK_MOE, CK_MOE), 0)
    acc = jnp.zeros(o_ref.shape, F32)
    for e in range(E):
        idx = c * E + e
        p = pos_ref[0, e:e + 1, :]
        w = comb_ref[0, e:e + 1, :]
        w0 = jnp.where(p == b0_ref[idx] * CK_MOE + r, w, 0.0).astype(BF)
        keep = (1 - dup_ref[idx]).astype(F32)
        w1 = jnp.where(p == b1_ref[idx] * CK_MOE + r, w * keep, 0.0).astype(BF)
        acc = acc + _dot_tn(w0, y_refs[2 * e][...])
        acc = acc + _dot_tn(w1, y_refs[2 * e + 1][...])
    o_ref[...] = x_ref[...] + gf_ref[0] * (_rms(acc) * post_ref[...])


def _moe_combine(x2, ys, plan, modr, l, post_g, B):
    n_tok, D = x2.shape
    nck = n_tok // CK_MOE
    ck_per_b = nck // B
    E = N_EXPERTS
    base = (l * N_ADA + 5) * B
    y_specs = []
    for e in range(E):
        y_specs.append(pl.BlockSpec((CK_MOE, D), lambda c, b0, b1, du, e=e: (b0[c * E + e], 0)))
        y_specs.append(pl.BlockSpec((CK_MOE, D), lambda c, b0, b1, du, e=e: (b1[c * E + e], 0)))
    grid_spec = pltpu.PrefetchScalarGridSpec(
        num_scalar_prefetch=3,
        grid=(nck,),
        in_specs=y_specs + [
            pl.BlockSpec((1, E, CK_MOE), lambda c, *_: (c, 0, 0)),
            pl.BlockSpec((1, E, CK_MOE), lambda c, *_: (c, 0, 0)),
            pl.BlockSpec((CK_MOE, D), lambda c, *_: (c, 0)),
            pl.BlockSpec((1, 1, D), lambda c, *_: (base + c // ck_per_b, 0, 0)),
            pl.BlockSpec((1, D), lambda c, *_: (0, 0)),
        ],
        out_specs=pl.BlockSpec((CK_MOE, D), lambda c, *_: (c, 0)),
    )
    return pl.pallas_call(
        _combine_kernel,
        grid_spec=grid_spec,
        out_shape=jax.ShapeDtypeStruct((n_tok, D), F32),
        compiler_params=_cparams(("arbitrary",)),
        name="moe_combine",
    )(plan["b0"], plan["b1"], plan["dup"], *([ys] * (2 * E)), plan["pos_ck"], plan["comb_ck"],
      x2, modr, post_g.reshape(1, D))


def _moe_layer(x, modr, l, pre_g, post_g, router_w, wg, wu, wd, lm):
    B, S, D = x.shape
    n_tok = B * S
    rw_p = jnp.zeros((D, LANES), F32).at[:, :N_EXPERTS].set(router_w)
    h, comb = _router(x, modr, l, pre_g, rw_p)
    plan = _moe_plan(comb.reshape(n_tok, LANES), n_tok)
    xs = _moe_gather(h.reshape(n_tok, D), plan)
    ys = _moe_ffn(xs, plan, wg, wu, wd, lm)
    out = _moe_combine(x.reshape(n_tok, D), ys, plan, modr, l, post_g, B)
    return out.reshape(B, S, D)


def _pad_w_in(w_in_l):
    split = C_CONV + C_SB + 4 * GLA_DIM + GLA_RANK
    D = w_in_l.shape[0]
    pad = jnp.zeros((D, LANES - GLA_RANK), w_in_l.dtype)
    return jnp.concatenate([w_in_l[:, :split], pad, w_in_l[:, split:]], axis=1).astype(BF)


def kernel(x, c, ada_w, ada_b, mix_pre_g, mix_post_g, ffn_pre_g, ffn_post_g, w_in, gla_a2, gla_a_b, conv_dw, conv_dw_b, conv_ln_g, conv_ln_b, gla_norm_g, w_conv_out, w_sb_out, w_gla_out, w_o, ffn_w_gate, ffn_w_up, ffn_w_down, router_w, moe_w_gate, moe_w_up, moe_w_down):
    depth = w_in.shape[0]
    modr = _ada_mod(c, ada_w, ada_b)
    for l in range(depth):
        glu, sb, gla, gates = _mixer_in(x, modr, l, mix_pre_g[l], _pad_w_in(w_in[l]))
        ca = _conv_branch(glu, conv_dw[l], conv_dw_b[l], conv_ln_g[l], conv_ln_b[l])
        ob = _sb_attention(sb)
        a2p = jnp.zeros((LANES, GLA_DIM), F32).at[:GLA_RANK].set(gla_a2[l]).astype(BF)
        oc = _gla_branch(gla, a2p, gla_a_b[l], gla_norm_g[l])
        x = _mixer_out(x, ca, ob, oc, gates, w_conv_out[l].astype(BF), w_sb_out[l].astype(BF),
                       w_gla_out[l].astype(BF), w_o[l].astype(BF), modr, l, mix_post_g[l])
        j = l // 2
        if l % 2 == 0:
            x = _ffn_dense(x, modr, l, ffn_pre_g[l], ffn_post_g[l], ffn_w_gate[j].astype(BF),
                           ffn_w_up[j].astype(BF), ffn_w_down[j].astype(BF))
        else:
            x = _moe_layer(x, modr, l, ffn_pre_g[l], ffn_post_g[l], router_w[j],
                           moe_w_gate, moe_w_up, moe_w_down, j)
    return x
```

```python
import functools

import jax
import jax.numpy as jnp
from jax import lax
from jax.experimental import pallas as pl
from jax.experimental.pallas import tpu as pltpu

BF = jnp.bfloat16
F32 = jnp.float32
I32 = jnp.int32

EPS = 1e-6
LOG2E = 1.4426950408889634
N_ADA = 6
CONV_DIM = 512
CONV_WIDTH = 31
CONV_HALO = 32
SB_HEADS = 8
SB_HEAD_DIM = 64
SB_DIM = SB_HEADS * SB_HEAD_DIM
SB_QSCALE = SB_HEAD_DIM ** -0.5 * LOG2E
SB_BLOCK = 128
SB_TQ = 512
GLA_HEADS = 4
GLA_DK = 128
GLA_DV = 128
GLA_DIM = GLA_HEADS * GLA_DK
GLA_RANK = 16
GLA_TAU = 16.0
GLA_CHUNK = 64
N_EXPERTS = 8
LANES = 128
SUBLANES = 8
VMEM_LIMIT = 56 * 1024 * 1024

TM_IN = 256
TS_CONV = 512
T_GLA = 512
TM_OUT = 512
TM_FFN = 512
TM_MOE = 1024
TF_MOE = 512
TG_MOE = 256
CK_MOE = 256


def _cparams(sem):
    return pltpu.CompilerParams(dimension_semantics=sem, vmem_limit_bytes=VMEM_LIMIT)


def _dot(a, b):
    return jnp.dot(a, b, preferred_element_type=F32)


def _dot_nt(a, b):
    return lax.dot_general(a, b, (((1,), (1,)), ((), ())), preferred_element_type=F32)


def _dot_tn(a, b):
    return lax.dot_general(a, b, (((0,), (0,)), ((), ())), preferred_element_type=F32)


def _sigmoid(x):
    return 1.0 / (1.0 + jnp.exp(-x))


def _log_sigmoid(x):
    return jnp.minimum(x, 0.0) - jnp.log(1.0 + jnp.exp2(jnp.abs(x) * (-LOG2E)))


def _rms(y):
    return y * lax.rsqrt(jnp.mean(y * y, axis=-1, keepdims=True) + EPS)


def _split_bf16(x):
    hi = x.astype(BF)
    lo = (x - hi.astype(F32)).astype(BF)
    return hi, lo


def _ada_kernel(c_ref, w_ref, b_ref, o_ref):
    c = c_ref[...]
    rows = c.shape[0]
    ch, cl = _split_bf16(c * _sigmoid(c))
    wh, wl = _split_bf16(w_ref[0])
    top = _dot(jnp.concatenate([ch, cl], axis=0), wh)
    o_ref[0] = top[:rows] + top[rows:] + _dot(ch, wl) + b_ref[0]


def _ada_mod(c, ada_w, ada_b):
    L, D, D6 = ada_w.shape
    B = c.shape[0]
    rows = 16
    cp = jnp.zeros((rows, D), F32).at[:B].set(c)
    tn = 1536
    out = pl.pallas_call(
        _ada_kernel,
        grid=(L, D6 // tn),
        in_specs=[
            pl.BlockSpec((rows, D), lambda l, j: (0, 0)),
            pl.BlockSpec((1, D, tn), lambda l, j: (l, 0, j)),
            pl.BlockSpec((1, 1, tn), lambda l, j: (l, 0, j)),
        ],
        out_specs=pl.BlockSpec((1, rows, tn), lambda l, j: (l, 0, j)),
        out_shape=jax.ShapeDtypeStruct((L, rows, D6), F32),
        compiler_params=_cparams(("parallel", "parallel")),
        name="ada_mod",
    )(cp, ada_w, ada_b.reshape(L, 1, D6))
    mod = out[:, :B].reshape(L, B, N_ADA, D).transpose(0, 2, 1, 3)
    return mod.reshape(L * N_ADA * B, 1, D)


def _mod_spec(l, k, B, D, batch_of):
    base = (l * N_ADA + k) * B
    return pl.BlockSpec((1, 1, D), lambda *g: (base + batch_of(*g), 0, 0))


def _prenorm(x, g, sc, sh):
    return _rms(x) * g * (1.0 + sc) + sh


C_CONV = 2 * CONV_DIM
C_SB = 3 * SB_DIM
C_GLA = 4 * GLA_DIM + LANES
C_GATE = 3 * 1024


def _mixer_in_kernel(x_ref, sc_ref, sh_ref, g_ref, w_ref, glu_ref, sb_ref, gla_ref, gate_ref):
    h = _prenorm(x_ref[0], g_ref[...], sc_ref[0], sh_ref[0]).astype(BF)
    o0 = 0
    u = _dot(h, w_ref[:, o0:o0 + C_CONV])
    glu_ref[0] = (u[:, :CONV_DIM] * _sigmoid(u[:, CONV_DIM:])).astype(BF)
    o0 += C_CONV
    sb_ref[0, :, 0:SB_DIM] = (_dot(h, w_ref[:, o0:o0 + SB_DIM]) * SB_QSCALE).astype(BF)
    sb_ref[0, :, SB_DIM:] = _dot(h, w_ref[:, o0 + SB_DIM:o0 + C_SB]).astype(BF)
    o0 += C_SB
    nq = 3 * GLA_DIM
    gla_ref[0, :, 0:nq] = _dot(h, w_ref[:, o0:o0 + nq]).astype(BF)
    gg = _dot(h, w_ref[:, o0 + nq:o0 + nq + GLA_DIM])
    gla_ref[0, :, nq:nq + GLA_DIM] = (gg * _sigmoid(gg)).astype(BF)
    gla_ref[0, :, nq + GLA_DIM:] = _dot(h, w_ref[:, o0 + nq + GLA_DIM:o0 + C_GLA]).astype(BF)
    o0 += C_GLA
    gate_ref[0] = _sigmoid(_dot(h, w_ref[:, o0:o0 + C_GATE])).astype(BF)


def _mixer_in(x, modr, l, pre_g, w_in_p):
    B, S, D = x.shape
    tm = min(TM_IN, S)
    NC = w_in_p.shape[1]
    bof = lambda b, i: b
    outs = pl.pallas_call(
        _mixer_in_kernel,
        grid=(B, S // tm),
        in_specs=[
            pl.BlockSpec((1, tm, D), lambda b, i: (b, i, 0)),
            _mod_spec(l, 1, B, D, bof),
            _mod_spec(l, 0, B, D, bof),
            pl.BlockSpec((1, D), lambda b, i: (0, 0)),
            pl.BlockSpec((D, NC), lambda b, i: (0, 0)),
        ],
        out_specs=[
            pl.BlockSpec((1, tm, CONV_DIM), lambda b, i: (b, i, 0)),
            pl.BlockSpec((1, tm, C_SB), lambda b, i: (b, i, 0)),
            pl.BlockSpec((1, tm, C_GLA), lambda b, i: (b, i, 0)),
            pl.BlockSpec((1, tm, C_GATE), lambda b, i: (b, i, 0)),
        ],
        out_shape=[
            jax.ShapeDtypeStruct((B, S, CONV_DIM), BF),
            jax.ShapeDtypeStruct((B, S, C_SB), BF),
            jax.ShapeDtypeStruct((B, S, C_GLA), BF),
            jax.ShapeDtypeStruct((B, S, C_GATE), BF),
        ],
        compiler_params=_cparams(("parallel", "parallel")),
        name="mixer_in",
    )(x, modr, modr, pre_g.reshape(1, D), w_in_p)
    return outs


def _conv_kernel(prev_ref, cur_ref, wdw_ref, bdw_ref, lng_ref, lnb_ref, o_ref, ycat_ref, ysh_ref,
                 *, ts):
    i = pl.program_id(1)
    prev = prev_ref[0].astype(F32)
    ycat_ref[0:CONV_HALO, :] = jnp.where(i > 0, prev, 0.0)
    ycat_ref[CONV_HALO:, :] = cur_ref[0].astype(F32)
    nsh = ysh_ref.shape[1]
    for s in range(1, SUBLANES):
        ysh_ref[s - 1] = ycat_ref[s:s + nsh, :]
    rc = 64
    first = CONV_HALO - (CONV_WIDTH - 1)
    for r0 in range(0, ts, rc):
        acc = jnp.zeros((rc, CONV_DIM), F32) + bdw_ref[...]
        for w in range(CONV_WIDTH):
            s = (first + w) % SUBLANES
            a0 = r0 + (first + w) - s
            win = ycat_ref[a0:a0 + rc, :] if s == 0 else ysh_ref[s - 1, a0:a0 + rc, :]
            acc = acc + wdw_ref[w:w + 1, :] * win
        mu = jnp.mean(acc, axis=-1, keepdims=True)
        d = acc - mu
        var = jnp.mean(d * d, axis=-1, keepdims=True)
        y = d * lax.rsqrt(var + EPS) * lng_ref[...] + lnb_ref[...]
        o_ref[0, r0:r0 + rc, :] = (y * _sigmoid(y)).astype(BF)


def _conv_branch(glu, wdw, bdw, lng, lnb):
    B, S, C = glu.shape
    ts = min(TS_CONV, S)
    hb = ts // CONV_HALO
    vec = lambda b, i: (0, 0)
    return pl.pallas_call(
        functools.partial(_conv_kernel, ts=ts),
        grid=(B, S // ts),
        in_specs=[
            pl.BlockSpec((1, CONV_HALO, C), lambda b, i: (b, jnp.maximum(i * hb - 1, 0), 0)),
            pl.BlockSpec((1, ts, C), lambda b, i: (b, i, 0)),
            pl.BlockSpec((CONV_WIDTH, C), vec),
            pl.BlockSpec((1, C), vec),
            pl.BlockSpec((1, C), vec),
            pl.BlockSpec((1, C), vec),
        ],
        out_specs=pl.BlockSpec((1, ts, C), lambda b, i: (b, i, 0)),
        out_shape=jax.ShapeDtypeStruct((B, S, C), BF),
        scratch_shapes=[pltpu.VMEM((ts + CONV_HALO, C), F32),
                        pltpu.VMEM((SUBLANES - 1, ts + CONV_HALO - SUBLANES, C), F32)],
        compiler_params=_cparams(("parallel", "parallel")),
        name="conv_branch",
    )(glu, glu, wdw, bdw.reshape(1, C), lng.reshape(1, C), lnb.reshape(1, C))


def _sb_kernel(q_ref, k_ref, v_ref, uo_ref, o_ref, acc_ref, carry_ref):
    i = pl.program_id(2)
    tb = SB_BLOCK
    tq = SB_TQ
    blocks = list(reversed(range(tq // tb)))
    lane = lax.broadcasted_iota(I32, (1, LANES), 1)
    head_masks = (lane < SB_HEAD_DIM, lane >= SB_HEAD_DIM)
    q = q_ref[0]
    uo = uo_ref[...]
    acc_ref[...] = jnp.zeros_like(acc_ref)
    carry_ref[...] = jnp.zeros_like(carry_ref)
    row = lax.broadcasted_iota(I32, (tb, tb), 0)
    col = lax.broadcasted_iota(I32, (tb, tb), 1)
    causal = col < row

    def mask_diag(a):
        top = jnp.where(causal, a[:tb], 0.0)
        return top if a.shape[0] == tb else jnp.concatenate([top, a[tb:]], axis=0)

    def group(base, band):
        r0s = {jb: (jb * tb if band else 0) for jb in blocks}
        zs, vvs = {}, {}
        for jb in blocks:
            start = pl.multiple_of(base + jb * tb, tb)
            k = k_ref[0, pl.ds(start, tb), :]
            v = v_ref[0, pl.ds(start, tb), :]
            vz = jnp.zeros_like(v)
            vvs[jb] = jnp.concatenate([jnp.where(m, v, vz) for m in head_masks], axis=0)
            kk = jnp.concatenate([jnp.where(m, k, vz) for m in head_masks], axis=0)
            zz = _dot_nt(q[r0s[jb]:], kk)
            for h in range(2):
                zs[jb, h] = zz[:, h * tb:(h + 1) * tb]
        lbs, rs = {}, {}
        for key, z in zs.items():
            nabs = pltpu.bitcast(pltpu.bitcast(z, I32) | jnp.int32(-2 ** 31), F32)
            lb = jnp.minimum(z, 0.0) - jnp.log2(1.0 + jnp.exp2(nabs))
            lk = lb - z
            if band:
                lk = mask_diag(lk)
            lbs[key] = lb
            rs[key] = _dot(lk.astype(BF), uo)
        car = [carry_ref[h] for h in range(2)]
        atts = {}
        for jb in blocks:
            r0 = r0s[jb]
            for h in range(2):
                r = rs[jb, h]
                att = jnp.exp2(lbs[jb, h] + r[:, :tb] + car[h][r0:])
                if band:
                    att = mask_diag(att)
                atts[jb, h] = att.astype(BF)
                upd = car[h][r0:] + r[:, tb:]
                car[h] = upd if r0 == 0 else jnp.concatenate([car[h][:r0], upd], axis=0)
        for h in range(2):
            carry_ref[h] = car[h]
        if band:
            for jb in blocks:
                both = jnp.concatenate([atts[jb, 0], atts[jb, 1]], axis=1)
                acc_ref[r0s[jb]:, :] += _dot(both, vvs[jb])
        else:
            both = jnp.concatenate([atts[jb, h] for jb in blocks for h in range(2)], axis=1)
            acc_ref[...] += _dot(both, jnp.concatenate([vvs[jb] for jb in blocks], axis=0))

    group(pl.multiple_of(i * tq, tq), True)

    def body(g, c):
        group(pl.multiple_of((i - 1 - g) * tq, tq), False)
        return c

    lax.fori_loop(0, i, body, 0)
    o_ref[0] = acc_ref[...].astype(BF)


def _sb_consts():
    tb = SB_BLOCK
    j = jnp.arange(tb)[:, None]
    s = jnp.arange(tb)[None, :]
    return jnp.concatenate([(j > s).astype(BF), jnp.ones((tb, tb), BF)], axis=1)


def _sb_attention(sb):
    B, S, _ = sb.shape
    tb = SB_BLOCK
    tq = SB_TQ
    npair = SB_DIM // LANES
    return pl.pallas_call(
        _sb_kernel,
        grid=(B, npair, S // tq),
        in_specs=[
            pl.BlockSpec((1, tq, LANES), lambda b, p, i: (b, i, p)),
            pl.BlockSpec((1, S, LANES), lambda b, p, i: (b, 0, npair + p)),
            pl.BlockSpec((1, S, LANES), lambda b, p, i: (b, 0, 2 * npair + p)),
            pl.BlockSpec((tb, 2 * tb), lambda b, p, i: (0, 0)),
        ],
        out_specs=pl.BlockSpec((1, tq, LANES), lambda b, p, i: (b, i, p)),
        out_shape=jax.ShapeDtypeStruct((B, S, SB_DIM), BF),
        scratch_shapes=[pltpu.VMEM((tq, LANES), F32), pltpu.VMEM((2, tq, LANES), F32)],
        compiler_params=_cparams(("parallel", "parallel", "parallel")),
        name="sb_attention",
    )(sb, sb, sb, _sb_consts())


def _gla_kernel(q_ref, k_ref, v_ref, gg_ref, glr_ref, a2_ref, ab_ref, ng_ref, lc_ref, mk_ref,
                o_ref, st_ref, *, t):
    @pl.when(pl.program_id(2) == 0)
    def _():
        st_ref[...] = jnp.zeros_like(st_ref)

    cs = GLA_CHUNK
    nc = t // cs
    scale = GLA_DK ** -0.5
    u = _dot(glr_ref[0], a2_ref[...]) + ab_ref[...]
    la = _log_sigmoid(u) * (1.0 / GLA_TAU)
    hi, lo = _split_bf16(la)
    lc = lc_ref[...]
    bs = [_dot(lc, jnp.concatenate([hi[c * cs:(c + 1) * cs], lo[c * cs:(c + 1) * cs]], axis=0))
          for c in range(nc)]
    b = jnp.concatenate(bs, axis=0)
    btot = jnp.concatenate([jnp.broadcast_to(bc[cs - 1:cs], (cs, GLA_DK)) for bc in bs], axis=0)
    q = q_ref[0].astype(F32) * scale
    k = k_ref[0].astype(F32)
    v = v_ref[0]
    qd = (q * jnp.exp(b)).astype(BF)
    ki = (k * jnp.exp(-b)).astype(BF)
    ke = (k * jnp.exp(btot - b)).astype(BF)
    sc = jnp.where(mk_ref[...] > 0.0, _dot_nt(qd, ki), 0.0).astype(BF)
    o = _dot(sc, v)
    ds = [_dot_tn(v[c * cs:(c + 1) * cs], ke[c * cs:(c + 1) * cs]) for c in range(nc)]
    st = st_ref[...]
    inter = []
    for c in range(nc):
        inter.append(_dot_nt(qd[c * cs:(c + 1) * cs], st.astype(BF)))
        st = st * jnp.exp(bs[c][cs - 1:cs]) + ds[c]
    st_ref[...] = st
    o = o + jnp.concatenate(inter, axis=0)
    o_ref[0] = (_rms(o) * ng_ref[...] * gg_ref[0].astype(F32)).astype(BF)


def _gla_consts(t):
    cs = GLA_CHUNK
    r = jnp.arange(t)[:, None]
    c = jnp.arange(t)[None, :]
    mask = ((r // cs) == (c // cs)) & (c <= r)
    low = mask[:cs, :cs].astype(BF)
    return jnp.concatenate([low, low], axis=1), mask.astype(F32)


def _gla_branch(gla, a2p, ab, ng):
    B, S, _ = gla.shape
    t = min(T_GLA, S)
    H = GLA_HEADS
    col = lambda off: (lambda b, h, s: (b, s, off + h))
    par = lambda b, h, s: (0, h)
    const = lambda b, h, s: (0, 0)
    lc, mk = _gla_consts(t)
    return pl.pallas_call(
        functools.partial(_gla_kernel, t=t),
        grid=(B, H, S // t),
        in_specs=[
            pl.BlockSpec((1, t, LANES), col(0)),
            pl.BlockSpec((1, t, LANES), col(H)),
            pl.BlockSpec((1, t, LANES), col(2 * H)),
            pl.BlockSpec((1, t, LANES), col(3 * H)),
            pl.BlockSpec((1, t, LANES), lambda b, h, s: (b, s, 4 * H)),
            pl.BlockSpec((LANES, GLA_DK), par),
            pl.BlockSpec((1, GLA_DK), par),
            pl.BlockSpec((1, GLA_DV), par),
            pl.BlockSpec((GLA_CHUNK, 2 * GLA_CHUNK), const),
            pl.BlockSpec((t, t), const),
        ],
        out_specs=pl.BlockSpec((1, t, LANES), lambda b, h, s: (b, s, h)),
        out_shape=jax.ShapeDtypeStruct((B, S, GLA_DIM), BF),
        scratch_shapes=[pltpu.VMEM((GLA_DV, GLA_DK), F32)],
        compiler_params=_cparams(("parallel", "parallel", "arbitrary")),
        name="gla_branch",
    )(gla, gla, gla, gla, gla, a2p, ab.reshape(1, GLA_DIM), ng.reshape(1, GLA_DIM), lc, mk)


def _mixer_out_kernel(x_ref, ca_ref, ob_ref, oc_ref, gt_ref, wc_ref, wb_ref, wg_ref, wo_ref,
                      gm_ref, pg_ref, o_ref):
    D = x_ref.shape[-1]
    gt = gt_ref[0]
    m = gt[:, 0:D].astype(F32) * _dot(ca_ref[0], wc_ref[...])
    m = m + gt[:, D:2 * D].astype(F32) * _dot(ob_ref[0], wb_ref[...])
    m = m + gt[:, 2 * D:3 * D].astype(F32) * _dot(oc_ref[0], wg_ref[...])
    y = _dot(m.astype(BF), wo_ref[...])
    o_ref[0] = x_ref[0] + gm_ref[0] * (_rms(y) * pg_ref[...])


def _mixer_out(x, ca, ob, oc, gates, wc, wb, wg, wo, modr, l, post_g):
    B, S, D = x.shape
    tm = min(TM_OUT, S)
    tok = lambda w: pl.BlockSpec((1, tm, w), lambda b, i: (b, i, 0))
    full = lambda a: pl.BlockSpec(a.shape, lambda b, i: (0, 0))
    return pl.pallas_call(
        _mixer_out_kernel,
        grid=(B, S // tm),
        in_specs=[tok(D), tok(CONV_DIM), tok(SB_DIM), tok(GLA_DIM), tok(C_GATE),
                  full(wc), full(wb), full(wg), full(wo),
                  _mod_spec(l, 2, B, D, lambda b, i: b),
                  pl.BlockSpec((1, D), lambda b, i: (0, 0))],
        out_specs=tok(D),
        out_shape=jax.ShapeDtypeStruct((B, S, D), F32),
        compiler_params=_cparams(("parallel", "parallel")),
        name="mixer_out",
    )(x, ca, ob, oc, gates, wc, wb, wg, wo, modr, post_g.reshape(1, D))


def _ffn_kernel(x_ref, sc_ref, sh_ref, pre_ref, wg_ref, wu_ref, wd_ref, gf_ref, post_ref, o_ref,
                h_ref, acc_ref):
    j = pl.program_id(2)

    @pl.when(j == 0)
    def _():
        h_ref[...] = _prenorm(x_ref[0], pre_ref[...], sc_ref[0], sh_ref[0]).astype(BF)
        acc_ref[...] = jnp.zeros_like(acc_ref)

    h = h_ref[...]
    g = _dot(h, wg_ref[...])
    u = _dot(h, wu_ref[...])
    acc_ref[...] += _dot((g * _sigmoid(g) * u).astype(BF), wd_ref[...])

    @pl.when(j == pl.num_programs(2) - 1)
    def _():
        o_ref[0] = x_ref[0] + gf_ref[0] * (_rms(acc_ref[...]) * post_ref[...])


def _ffn_dense(x, modr, l, pre_g, post_g, wg, wu, wd):
    B, S, D = x.shape
    F = wg.shape[1]
    tm = min(TM_FFN, S)
    tf = F // 2 if (F // 2) % LANES == 0 else F
    bof = lambda b, i, j: b
    return pl.pallas_call(
        _ffn_kernel,
        grid=(B, S // tm, F // tf),
        in_specs=[
            pl.BlockSpec((1, tm, D), lambda b, i, j: (b, i, 0)),
            _mod_spec(l, 4, B, D, bof),
            _mod_spec(l, 3, B, D, bof),
            pl.BlockSpec((1, D), lambda b, i, j: (0, 0)),
            pl.BlockSpec((D, tf), lambda b, i, j: (0, j)),
            pl.BlockSpec((D, tf), lambda b, i, j: (0, j)),
            pl.BlockSpec((tf, D), lambda b, i, j: (j, 0)),
            _mod_spec(l, 5, B, D, bof),
            pl.BlockSpec((1, D), lambda b, i, j: (0, 0)),
        ],
        out_specs=pl.BlockSpec((1, tm, D), lambda b, i, j: (b, i, 0)),
        out_shape=jax.ShapeDtypeStruct((B, S, D), F32),
        scratch_shapes=[pltpu.VMEM((tm, D), BF), pltpu.VMEM((tm, D), F32)],
        compiler_params=_cparams(("parallel", "parallel", "arbitrary")),
        name="ffn_dense",
    )(x, modr, modr, pre_g.reshape(1, D), wg, wu, wd, modr, post_g.reshape(1, D))


def _router_kernel(x_ref, sc_ref, sh_ref, pre_ref, rw_ref, h_ref, comb_ref):
    h = _prenorm(x_ref[0], pre_ref[...], sc_ref[0], sh_ref[0])
    h_ref[0] = h.astype(BF)
    hh, hl = _split_bf16(h)
    rh, rl = _split_bf16(rw_ref[...])
    lg = _dot(hh, rh) + _dot(hh, rl) + _dot(hl, rh)
    lane = lax.broadcasted_iota(I32, lg.shape, 1).astype(F32)
    ninf = jnp.float32(-jnp.inf)
    lg = jnp.where(lane < N_EXPERTS, lg, ninf)
    m1 = jnp.max(lg, axis=-1, keepdims=True)
    i1 = jnp.min(jnp.where(lg == m1, lane, float(LANES)), axis=-1, keepdims=True)
    lg2 = jnp.where(lane == i1, ninf, lg)
    m2 = jnp.max(lg2, axis=-1, keepdims=True)
    i2 = jnp.min(jnp.where(lg2 == m2, lane, float(LANES)), axis=-1, keepdims=True)
    e = jnp.exp(m2 - m1)
    p1 = 1.0 / (1.0 + e)
    comb_ref[0] = jnp.where(lane == i1, p1, jnp.where(lane == i2, e * p1, -1.0))


def _router(x, modr, l, pre_g, rw_p):
    B, S, D = x.shape
    tm = min(TM_FFN, S)
    bof = lambda b, i: b
    return pl.pallas_call(
        _router_kernel,
        grid=(B, S // tm),
        in_specs=[
            pl.BlockSpec((1, tm, D), lambda b, i: (b, i, 0)),
            _mod_spec(l, 4, B, D, bof),
            _mod_spec(l, 3, B, D, bof),
            pl.BlockSpec((1, D), lambda b, i: (0, 0)),
            pl.BlockSpec((D, LANES), lambda b, i: (0, 0)),
        ],
        out_specs=[pl.BlockSpec((1, tm, D), lambda b, i: (b, i, 0)),
                   pl.BlockSpec((1, tm, LANES), lambda b, i: (b, i, 0))],
        out_shape=[jax.ShapeDtypeStruct((B, S, D), BF), jax.ShapeDtypeStruct((B, S, LANES), F32)],
        compiler_params=_cparams(("parallel", "parallel")),
        name="moe_router",
    )(x, modr, modr, pre_g.reshape(1, D), rw_p)


def _moe_plan(comb, n_tok):
    E = N_EXPERTS
    nck = n_tok // CK_MOE
    sel = comb[:, :E] >= 0.0
    seli = sel.astype(I32)
    rank_incl = jnp.cumsum(seli, axis=0)
    cnt = rank_incl[-1]
    padded = ((cnt + TM_MOE - 1) // TM_MOE) * TM_MOE
    seg_end = jnp.cumsum(padded)
    seg_start = seg_end - padded
    total = seg_end[-1]
    pos = jnp.where(sel, seg_start[None, :] + rank_incl - seli, -1)
    pos_ck = pos.reshape(nck, CK_MOE, E).transpose(0, 2, 1)
    comb_ck = jnp.where(sel, comb[:, :E], 0.0).reshape(nck, CK_MOE, E).transpose(0, 2, 1)
    cnt_ck = seli.reshape(nck, CK_MOE, E).sum(axis=1)
    cex = jnp.concatenate([jnp.zeros((1, E), I32), jnp.cumsum(cnt_ck, axis=0)], axis=0)

    rows_pad = 2 * n_tok + E * TM_MOE
    jt = jnp.arange(rows_pad // TG_MOE, dtype=I32) * TG_MOE
    te = jnp.minimum(jnp.sum(seg_end[None, :] <= jt[:, None], axis=1), E - 1).astype(I32)
    r0 = jt - seg_start[te]
    r1 = jnp.minimum(r0 + TG_MOE, cnt[te])
    valid = (jt < total) & (r1 > r0)
    cte = cex[1:, :][:, te]
    c_lo = jnp.sum(cte <= r0[None, :], axis=0).astype(I32)
    c_hi = jnp.sum(cte <= (r1 - 1)[None, :], axis=0).astype(I32)
    c_lo = jnp.where(valid, c_lo, 1)
    c_hi = jnp.where(valid, jnp.minimum(c_hi, nck - 1), 0)
    jf = jnp.arange(rows_pad // TM_MOE, dtype=I32) * TM_MOE
    fe = jnp.minimum(jnp.sum(seg_end[None, :] <= jf[:, None], axis=1), E - 1).astype(I32)
    real = jnp.clip(seg_start[fe] + cnt[fe] - jf, 0, TM_MOE)
    hm = TM_MOE // 2
    fvalid = jnp.where(jf < total, (real + hm - 1) // hm, 0).astype(I32)
    w0 = seg_start[None, :] + cex[:-1, :]
    nb = rows_pad // CK_MOE
    b0 = jnp.minimum(w0 // CK_MOE, nb - 1).astype(I32)
    b1 = jnp.minimum(b0 + 1, nb - 1).astype(I32)
    dup = (b1 == b0).astype(I32)
    return dict(pos_ck=pos_ck, comb_ck=comb_ck, te=te, c_lo=c_lo, c_hi=c_hi, fe=fe, fvalid=fvalid,
                b0=b0.reshape(-1), b1=b1.reshape(-1), dup=dup.reshape(-1), rows_pad=rows_pad)


def _gather_kernel(te_ref, clo_ref, chi_ref, h_ref, pos_ref, o_ref, acc_ref):
    j = pl.program_id(0)
    e = te_ref[j]
    rows = j * TG_MOE + lax.broadcasted_iota(I32, (TG_MOE, CK_MOE), 0)
    acc_ref[...] = jnp.zeros_like(acc_ref)

    def body(c, carry):
        p = pos_ref[c, pl.ds(e, 1), :]
        onehot = jnp.where(rows == p, 1.0, 0.0).astype(BF)
        start = pl.multiple_of(c * CK_MOE, CK_MOE)
        acc_ref[...] += _dot(onehot, h_ref[pl.ds(start, CK_MOE), :])
        return carry

    lax.fori_loop(clo_ref[j], chi_ref[j] + 1, body, 0)
    o_ref[...] = acc_ref[...].astype(BF)


def _moe_gather(h2, plan):
    n_tok, D = h2.shape
    rows_pad = plan["rows_pad"]
    grid_spec = pltpu.PrefetchScalarGridSpec(
        num_scalar_prefetch=3,
        grid=(rows_pad // TG_MOE,),
        in_specs=[pl.BlockSpec(memory_space=pltpu.VMEM), pl.BlockSpec(memory_space=pltpu.VMEM)],
        out_specs=pl.BlockSpec((TG_MOE, D), lambda j, *_: (j, 0)),
        scratch_shapes=[pltpu.VMEM((TG_MOE, D), F32)],
    )
    return pl.pallas_call(
        _gather_kernel,
        grid_spec=grid_spec,
        out_shape=jax.ShapeDtypeStruct((rows_pad, D), BF),
        compiler_params=_cparams(("arbitrary",)),
        name="moe_gather",
    )(plan["te"], plan["c_lo"], plan["c_hi"], h2, plan["pos_ck"])


def _moe_ffn_kernel(fe_ref, fv_ref, x_ref, wg_ref, wu_ref, wd_ref, o_ref, acc_ref):
    j = pl.program_id(0)
    f = pl.program_id(1)

    @pl.when(f == 0)
    def _():
        acc_ref[...] = jnp.zeros_like(acc_ref)

    hm = TM_MOE // 2
    for half in range(2):
        @pl.when(fv_ref[j] > half)
        def _(half=half):
            rows = slice(half * hm, (half + 1) * hm)
            x = x_ref[rows, :]
            g = _dot(x, wg_ref[0, 0].astype(BF))
            u = _dot(x, wu_ref[0, 0].astype(BF))
            acc_ref[rows, :] += _dot((g * _sigmoid(g) * u).astype(BF), wd_ref[0, 0].astype(BF))

    @pl.when(f == pl.num_programs(1) - 1)
    def _():
        o_ref[...] = acc_ref[...].astype(BF)


def _moe_ffn(xs, plan, wg, wu, wd, lm):
    rows_pad, D = xs.shape
    F = wg.shape[3]
    tf = TF_MOE if F % TF_MOE == 0 else F
    nf = F // tf
    def fidx(f, fv, j):
        live = jnp.minimum(fv[j], 1)
        return f * live + (nf - 1) * (1 - live)

    grid_spec = pltpu.PrefetchScalarGridSpec(
        num_scalar_prefetch=2,
        grid=(rows_pad // TM_MOE, nf),
        in_specs=[
            pl.BlockSpec((TM_MOE, D), lambda j, f, fe, fv: (j, 0)),
            pl.BlockSpec((1, 1, D, tf), lambda j, f, fe, fv: (lm, fe[j], 0, fidx(f, fv, j))),
            pl.BlockSpec((1, 1, D, tf), lambda j, f, fe, fv: (lm, fe[j], 0, fidx(f, fv, j))),
            pl.BlockSpec((1, 1, tf, D), lambda j, f, fe, fv: (lm, fe[j], fidx(f, fv, j), 0)),
        ],
        out_specs=pl.BlockSpec((TM_MOE, D), lambda j, f, fe, fv: (j, 0)),
        scratch_shapes=[pltpu.VMEM((TM_MOE, D), F32)],
    )
    return pl.pallas_call(
        _moe_ffn_kernel,
        grid_spec=grid_spec,
        out_shape=jax.ShapeDtypeStruct((rows_pad, D), BF),
        compiler_params=_cparams(("arbitrary", "arbitrary")),
        name="moe_ffn",
    )(plan["fe"], plan["fvalid"], xs, wg, wu, wd)


def _combine_kernel(b0_ref, b1_ref, dup_ref, *refs):
    E = N_EXPERTS
    y_refs = refs[:2 * E]
    pos_ref, comb_ref, x_ref, gf_ref, post_ref, o_ref = refs[2 * E:]
    c = pl.program_id(0)
    r = lax.broadcasted_iota(I32, (CK_MOE, CK_MOE), 0)
    acc = jnp.zeros(o_ref.shape, F32)
    for e in range(E):
        idx = c * E + e
        p = pos_ref[0, e:e + 1, :]
        w = comb_ref[0, e:e + 1, :]
        w0 = jnp.where(p == b0_ref[idx] * CK_MOE + r, w, 0.0).astype(BF)
        keep = (1 - dup_ref[idx]).astype(F32)
        w1 = jnp.where(p == b1_ref[idx] * CK_MOE + r, w * keep, 0.0).astype(BF)
        acc = acc + _dot_tn(w0, y_refs[2 * e][...])
        acc = acc + _dot_tn(w1, y_refs[2 * e + 1][...])
    o_ref[...] = x_ref[...] + gf_ref[0] * (_rms(acc) * post_ref[...])


def _moe_combine(x2, ys, plan, modr, l, post_g, B):
    n_tok, D = x2.shape
    nck = n_tok // CK_MOE
    ck_per_b = nck // B
    E = N_EXPERTS
    base = (l * N_ADA + 5) * B
    y_specs = []
    for e in range(E):
        y_specs.append(pl.BlockSpec((CK_MOE, D), lambda c, b0, b1, du, e=e: (b0[c * E + e], 0)))
        y_specs.append(pl.BlockSpec((CK_MOE, D), lambda c, b0, b1, du, e=e: (b1[c * E + e], 0)))
    grid_spec = pltpu.PrefetchScalarGridSpec(
        num_scalar_prefetch=3,
        grid=(nck,),
        in_specs=y_specs + [
            pl.BlockSpec((1, E, CK_MOE), lambda c, *_: (c, 0, 0)),
            pl.BlockSpec((1, E, CK_MOE), lambda c, *_: (c, 0, 0)),
            pl.BlockSpec((CK_MOE, D), lambda c, *_: (c, 0)),
            pl.BlockSpec((1, 1, D), lambda c, *_: (base + c // ck_per_b, 0, 0)),
            pl.BlockSpec((1, D), lambda c, *_: (0, 0)),
        ],
        out_specs=pl.BlockSpec((CK_MOE, D), lambda c, *_: (c, 0)),
    )
    return pl.pallas_call(
        _combine_kernel,
        grid_spec=grid_spec,
        out_shape=jax.ShapeDtypeStruct((n_tok, D), F32),
        compiler_params=_cparams(("arbitrary",)),
        name="moe_combine",
    )(plan["b0"], plan["b1"], plan["dup"], *([ys] * (2 * E)), plan["pos_ck"], plan["comb_ck"],
      x2, modr, post_g.reshape(1, D))


def _moe_layer(x, modr, l, pre_g, post_g, router_w, wg, wu, wd, lm):
    B, S, D = x.shape
    n_tok = B * S
    rw_p = jnp.zeros((D, LANES), F32).at[:, :N_EXPERTS].set(router_w)
    h, comb = _router(x, modr, l, pre_g, rw_p)
    plan = _moe_plan(comb.reshape(n_tok, LANES), n_tok)
    xs = _moe_gather(h.reshape(n_tok, D), plan)
    ys = _moe_ffn(xs, plan, wg, wu, wd, lm)
    out = _moe_combine(x.reshape(n_tok, D), ys, plan, modr, l, post_g, B)
    return out.reshape(B, S, D)


def _pad_w_in(w_in_l):
    split = C_CONV + C_SB + 4 * GLA_DIM + GLA_RANK
    D = w_in_l.shape[0]
    pad = jnp.zeros((D, LANES - GLA_RANK), w_in_l.dtype)
    return jnp.concatenate([w_in_l[:, :split], pad, w_in_l[:, split:]], axis=1).astype(BF)


def kernel(x, c, ada_w, ada_b, mix_pre_g, mix_post_g, ffn_pre_g, ffn_post_g, w_in, gla_a2, gla_a_b, conv_dw, conv_dw_b, conv_ln_g, conv_ln_b, gla_norm_g, w_conv_out, w_sb_out, w_gla_out, w_o, ffn_w_gate, ffn_w_up, ffn_w_down, router_w, moe_w_gate, moe_w_up, moe_w_down):
    depth = w_in.shape[0]
    modr = _ada_mod(c, ada_w, ada_b)
    for l in range(depth):
        glu, sb, gla, gates = _mixer_in(x, modr, l, mix_pre_g[l], _pad_w_in(w_in[l]))
        ca = _conv_branch(glu, conv_dw[l], conv_dw_b[l], conv_ln_g[l], conv_ln_b[l])
        ob = _sb_attention(sb)
        a2p = jnp.zeros((LANES, GLA_DIM), F32).at[:GLA_RANK].set(gla_a2[l]).astype(BF)
        oc = _gla_branch(gla, a2p, gla_a_b[l], gla_norm_g[l])
        x = _mixer_out(x, ca, ob, oc, gates, w_conv_out[l].astype(BF), w_sb_out[l].astype(BF),
                       w_gla_out[l].astype(BF), w_o[l].astype(BF), modr, l, mix_post_g[l])
        j = l // 2
        if l % 2 == 0:
            x = _ffn_dense(x, modr, l, ffn_pre_g[l], ffn_post_g[l], ffn_w_gate[j].astype(BF),
                           ffn_w_up[j].astype(BF), ffn_w_down[j].astype(BF))
        else:
            x = _moe_layer(x, modr, l, ffn_pre_g[l], ffn_post_g[l], router_w[j],
                           moe_w_gate, moe_w_up, moe_w_down, j)
    return x
```

```python
import functools

import jax
import jax.numpy as jnp
from jax import lax
from jax.experimental import pallas as pl
from jax.experimental.pallas import tpu as pltpu

BF = jnp.bfloat16
F32 = jnp.float32
I32 = jnp.int32

EPS = 1e-6
LOG2E = 1.4426950408889634
N_ADA = 6
CONV_DIM = 512
CONV_WIDTH = 31
CONV_HALO = 32
SB_HEADS = 8
SB_HEAD_DIM = 64
SB_DIM = SB_HEADS * SB_HEAD_DIM
SB_QSCALE = SB_HEAD_DIM ** -0.5 * LOG2E
SB_BLOCK = 128
SB_TQ = 512
GLA_HEADS = 4
GLA_DK = 128
GLA_DV = 128
GLA_DIM = GLA_HEADS * GLA_DK
GLA_RANK = 16
GLA_TAU = 16.0
GLA_CHUNK = 64
N_EXPERTS = 8
LANES = 128
SUBLANES = 8
VMEM_LIMIT = 56 * 1024 * 1024

TM_IN = 512
TS_CONV = 512
T_GLA = 512
TM_OUT = 512
TM_FFN = 512
TM_MOE = 1024
TF_MOE = 512
TG_MOE = 256
CK_MOE = 256


def _cparams(sem):
    return pltpu.CompilerParams(dimension_semantics=sem, vmem_limit_bytes=VMEM_LIMIT)


def _dot(a, b):
    return jnp.dot(a, b, preferred_element_type=F32)


def _dot_nt(a, b):
    return lax.dot_general(a, b, (((1,), (1,)), ((), ())), preferred_element_type=F32)


def _dot_tn(a, b):
    return lax.dot_general(a, b, (((0,), (0,)), ((), ())), preferred_element_type=F32)


def _sigmoid(x):
    return 1.0 / (1.0 + jnp.exp(-x))


def _log_sigmoid(x):
    return jnp.minimum(x, 0.0) - jnp.log(1.0 + jnp.exp2(jnp.abs(x) * (-LOG2E)))


def _rms(y):
    return y * lax.rsqrt(jnp.mean(y * y, axis=-1, keepdims=True) + EPS)


def _split_bf16(x):
    hi = x.astype(BF)
    lo = (x - hi.astype(F32)).astype(BF)
    return hi, lo


def _ada_kernel(c_ref, w_ref, b_ref, o_ref):
    c = c_ref[...]
    rows = c.shape[0]
    ch, cl = _split_bf16(c * _sigmoid(c))
    wh, wl = _split_bf16(w_ref[0])
    top = _dot(jnp.concatenate([ch, cl], axis=0), wh)
    o_ref[0] = top[:rows] + top[rows:] + _dot(ch, wl) + b_ref[0]


def _ada_mod(c, ada_w, ada_b):
    L, D, D6 = ada_w.shape
    B = c.shape[0]
    rows = 16
    cp = jnp.zeros((rows, D), F32).at[:B].set(c)
    tn = 1536
    out = pl.pallas_call(
        _ada_kernel,
        grid=(L, D6 // tn),
        in_specs=[
            pl.BlockSpec((rows, D), lambda l, j: (0, 0)),
            pl.BlockSpec((1, D, tn), lambda l, j: (l, 0, j)),
            pl.BlockSpec((1, 1, tn), lambda l, j: (l, 0, j)),
        ],
        out_specs=pl.BlockSpec((1, rows, tn), lambda l, j: (l, 0, j)),
        out_shape=jax.ShapeDtypeStruct((L, rows, D6), F32),
        compiler_params=_cparams(("parallel", "parallel")),
        name="ada_mod",
    )(cp, ada_w, ada_b.reshape(L, 1, D6))
    mod = out[:, :B].reshape(L, B, N_ADA, D).transpose(0, 2, 1, 3)
    return mod.reshape(L * N_ADA * B, 1, D)


def _mod_spec(l, k, B, D, batch_of):
    base = (l * N_ADA + k) * B
    return pl.BlockSpec((1, 1, D), lambda *g: (base + batch_of(*g), 0, 0))


def _prenorm(x, g, sc, sh):
    return _rms(x) * g * (1.0 + sc) + sh


C_CONV = 2 * CONV_DIM
C_SB = 3 * SB_DIM
C_GLA = 4 * GLA_DIM + LANES
C_GATE = 3 * 1024


def _mixer_in_kernel(x_ref, sc_ref, sh_ref, g_ref, w_ref, glu_ref, sb_ref, gla_ref, gate_ref):
    h = _prenorm(x_ref[0], g_ref[...], sc_ref[0], sh_ref[0]).astype(BF)
    o0 = 0
    u = _dot(h, w_ref[:, o0:o0 + C_CONV])
    glu_ref[0] = (u[:, :CONV_DIM] * _sigmoid(u[:, CONV_DIM:])).astype(BF)
    o0 += C_CONV
    sb_ref[0, :, 0:SB_DIM] = (_dot(h, w_ref[:, o0:o0 + SB_DIM]) * SB_QSCALE).astype(BF)
    sb_ref[0, :, SB_DIM:] = _dot(h, w_ref[:, o0 + SB_DIM:o0 + C_SB]).astype(BF)
    o0 += C_SB
    nq = 3 * GLA_DIM
    gla_ref[0, :, 0:nq] = _dot(h, w_ref[:, o0:o0 + nq]).astype(BF)
    gg = _dot(h, w_ref[:, o0 + nq:o0 + nq + GLA_DIM])
    gla_ref[0, :, nq:nq + GLA_DIM] = (gg * _sigmoid(gg)).astype(BF)
    gla_ref[0, :, nq + GLA_DIM:] = _dot(h, w_ref[:, o0 + nq + GLA_DIM:o0 + C_GLA]).astype(BF)
    o0 += C_GLA
    gate_ref[0] = _sigmoid(_dot(h, w_ref[:, o0:o0 + C_GATE])).astype(BF)


def _mixer_in(x, modr, l, pre_g, w_in_p):
    B, S, D = x.shape
    tm = min(TM_IN, S)
    NC = w_in_p.shape[1]
    bof = lambda b, i: b
    outs = pl.pallas_call(
        _mixer_in_kernel,
        grid=(B, S // tm),
        in_specs=[
            pl.BlockSpec((1, tm, D), lambda b, i: (b, i, 0)),
            _mod_spec(l, 1, B, D, bof),
            _mod_spec(l, 0, B, D, bof),
            pl.BlockSpec((1, D), lambda b, i: (0, 0)),
            pl.BlockSpec((D, NC), lambda b, i: (0, 0), pipeline_mode=pl.Buffered(1)),
        ],
        out_specs=[
            pl.BlockSpec((1, tm, CONV_DIM), lambda b, i: (b, i, 0)),
            pl.BlockSpec((1, tm, C_SB), lambda b, i: (b, i, 0)),
            pl.BlockSpec((1, tm, C_GLA), lambda b, i: (b, i, 0)),
            pl.BlockSpec((1, tm, C_GATE), lambda b, i: (b, i, 0)),
        ],
        out_shape=[
            jax.ShapeDtypeStruct((B, S, CONV_DIM), BF),
            jax.ShapeDtypeStruct((B, S, C_SB), BF),
            jax.ShapeDtypeStruct((B, S, C_GLA), BF),
            jax.ShapeDtypeStruct((B, S, C_GATE), BF),
        ],
        compiler_params=_cparams(("parallel", "parallel")),
        name="mixer_in",
    )(x, modr, modr, pre_g.reshape(1, D), w_in_p)
    return outs


def _conv_kernel(prev_ref, cur_ref, wdw_ref, bdw_ref, lng_ref, lnb_ref, o_ref, ycat_ref, ysh_ref,
                 *, ts):
    i = pl.program_id(1)
    prev = prev_ref[0].astype(F32)
    ycat_ref[0:CONV_HALO, :] = jnp.where(i > 0, prev, 0.0)
    ycat_ref[CONV_HALO:, :] = cur_ref[0].astype(F32)
    nsh = ysh_ref.shape[1]
    for s in range(1, SUBLANES):
        ysh_ref[s - 1] = ycat_ref[s:s + nsh, :]
    rc = 64
    first = CONV_HALO - (CONV_WIDTH - 1)
    for r0 in range(0, ts, rc):
        acc = jnp.zeros((rc, CONV_DIM), F32) + bdw_ref[...]
        for w in range(CONV_WIDTH):
            s = (first + w) % SUBLANES
            a0 = r0 + (first + w) - s
            win = ycat_ref[a0:a0 + rc, :] if s == 0 else ysh_ref[s - 1, a0:a0 + rc, :]
            acc = acc + wdw_ref[w:w + 1, :] * win
        mu = jnp.mean(acc, axis=-1, keepdims=True)
        d = acc - mu
        var = jnp.mean(d * d, axis=-1, keepdims=True)
        y = d * lax.rsqrt(var + EPS) * lng_ref[...] + lnb_ref[...]
        o_ref[0, r0:r0 + rc, :] = (y * _sigmoid(y)).astype(BF)


def _conv_branch(glu, wdw, bdw, lng, lnb):
    B, S, C = glu.shape
    ts = min(TS_CONV, S)
    hb = ts // CONV_HALO
    vec = lambda b, i: (0, 0)
    return pl.pallas_call(
        functools.partial(_conv_kernel, ts=ts),
        grid=(B, S // ts),
        in_specs=[
            pl.BlockSpec((1, CONV_HALO, C), lambda b, i: (b, jnp.maximum(i * hb - 1, 0), 0)),
            pl.BlockSpec((1, ts, C), lambda b, i: (b, i, 0)),
            pl.BlockSpec((CONV_WIDTH, C), vec),
            pl.BlockSpec((1, C), vec),
            pl.BlockSpec((1, C), vec),
            pl.BlockSpec((1, C), vec),
        ],
        out_specs=pl.BlockSpec((1, ts, C), lambda b, i: (b, i, 0)),
        out_shape=jax.ShapeDtypeStruct((B, S, C), BF),
        scratch_shapes=[pltpu.VMEM((ts + CONV_HALO, C), F32),
                        pltpu.VMEM((SUBLANES - 1, ts + CONV_HALO - SUBLANES, C), F32)],
        compiler_params=_cparams(("parallel", "parallel")),
        name="conv_branch",
    )(glu, glu, wdw, bdw.reshape(1, C), lng.reshape(1, C), lnb.reshape(1, C))


def _sb_kernel(q_ref, k_ref, v_ref, uo_ref, o_ref, acc_ref, carry_ref):
    i = pl.program_id(2)
    tb = SB_BLOCK
    tq = SB_TQ
    blocks = list(reversed(range(tq // tb)))
    lane = lax.broadcasted_iota(I32, (1, LANES), 1)
    head_masks = (lane < SB_HEAD_DIM, lane >= SB_HEAD_DIM)
    q = q_ref[0]
    uo = uo_ref[...]
    acc_ref[...] = jnp.zeros_like(acc_ref)
    carry_ref[...] = jnp.zeros_like(carry_ref)
    row = lax.broadcasted_iota(I32, (tb, tb), 0)
    col = lax.broadcasted_iota(I32, (tb, tb), 1)
    causal = col < row

    def mask_diag(a):
        top = jnp.where(causal, a[:tb], 0.0)
        return top if a.shape[0] == tb else jnp.concatenate([top, a[tb:]], axis=0)

    def group(base, band):
        r0s = {jb: (jb * tb if band else 0) for jb in blocks}
        zs, vvs = {}, {}
        for jb in blocks:
            start = pl.multiple_of(base + jb * tb, tb)
            k = k_ref[0, pl.ds(start, tb), :]
            v = v_ref[0, pl.ds(start, tb), :]
            vz = jnp.zeros_like(v)
            vvs[jb] = jnp.concatenate([jnp.where(m, v, vz) for m in head_masks], axis=0)
            kk = jnp.concatenate([jnp.where(m, k, vz) for m in head_masks], axis=0)
            zz = _dot_nt(q[r0s[jb]:], kk)
            for h in range(2):
                zs[jb, h] = zz[:, h * tb:(h + 1) * tb]
        lbs, rs = {}, {}
        for key, z in zs.items():
            nabs = pltpu.bitcast(pltpu.bitcast(z, I32) | jnp.int32(-2 ** 31), F32)
            lb = jnp.minimum(z, 0.0) - jnp.log2(1.0 + jnp.exp2(nabs))
            lk = lb - z
            if band:
                lk = mask_diag(lk)
            lbs[key] = lb
            rs[key] = _dot(lk.astype(BF), uo)
        car = [carry_ref[h] for h in range(2)]
        atts = {}
        for jb in blocks:
            r0 = r0s[jb]
            for h in range(2):
                r = rs[jb, h]
                att = jnp.exp2(lbs[jb, h] + r[:, :tb] + car[h][r0:])
                if band:
                    att = mask_diag(att)
                atts[jb, h] = att.astype(BF)
                upd = car[h][r0:] + r[:, tb:]
                car[h] = upd if r0 == 0 else jnp.concatenate([car[h][:r0], upd], axis=0)
        for h in range(2):
            carry_ref[h] = car[h]
        if band:
            for jb in blocks:
                both = jnp.concatenate([atts[jb, 0], atts[jb, 1]], axis=1)
                acc_ref[r0s[jb]:, :] += _dot(both, vvs[jb])
        else:
            both = jnp.concatenate([atts[jb, h] for jb in blocks for h in range(2)], axis=1)
            acc_ref[...] += _dot(both, jnp.concatenate([vvs[jb] for jb in blocks], axis=0))

    group(pl.multiple_of(i * tq, tq), True)

    def body(g, c):
        group(pl.multiple_of((i - 1 - g) * tq, tq), False)
        return c

    lax.fori_loop(0, i, body, 0)
    o_ref[0] = acc_ref[...].astype(BF)


def _sb_consts():
    tb = SB_BLOCK
    j = jnp.arange(tb)[:, None]
    s = jnp.arange(tb)[None, :]
    return jnp.concatenate([(j > s).astype(BF), jnp.ones((tb, tb), BF)], axis=1)


def _sb_attention(sb):
    B, S, _ = sb.shape
    tb = SB_BLOCK
    tq = SB_TQ
    npair = SB_DIM // LANES
    return pl.pallas_call(
        _sb_kernel,
        grid=(B, npair, S // tq),
        in_specs=[
            pl.BlockSpec((1, tq, LANES), lambda b, p, i: (b, i, p)),
            pl.BlockSpec((1, S, LANES), lambda b, p, i: (b, 0, npair + p)),
            pl.BlockSpec((1, S, LANES), lambda b, p, i: (b, 0, 2 * npair + p)),
            pl.BlockSpec((tb, 2 * tb), lambda b, p, i: (0, 0)),
        ],
        out_specs=pl.BlockSpec((1, tq, LANES), lambda b, p, i: (b, i, p)),
        out_shape=jax.ShapeDtypeStruct((B, S, SB_DIM), BF),
        scratch_shapes=[pltpu.VMEM((tq, LANES), F32), pltpu.VMEM((2, tq, LANES), F32)],
        compiler_params=_cparams(("parallel", "parallel", "parallel")),
        name="sb_attention",
    )(sb, sb, sb, _sb_consts())


def _gla_kernel(q_ref, k_ref, v_ref, gg_ref, glr_ref, a2_ref, ab_ref, ng_ref, lc_ref, mk_ref,
                o_ref, st_ref, *, t):
    @pl.when(pl.program_id(2) == 0)
    def _():
        st_ref[...] = jnp.zeros_like(st_ref)

    cs = GLA_CHUNK
    nc = t // cs
    scale = GLA_DK ** -0.5
    u = _dot(glr_ref[0], a2_ref[...]) + ab_ref[...]
    la = _log_sigmoid(u) * (1.0 / GLA_TAU)
    hi, lo = _split_bf16(la)
    lc = lc_ref[...]
    bs = [_dot(lc, jnp.concatenate([hi[c * cs:(c + 1) * cs], lo[c * cs:(c + 1) * cs]], axis=0))
          for c in range(nc)]
    b = jnp.concatenate(bs, axis=0)
    btot = jnp.concatenate([jnp.broadcast_to(bc[cs - 1:cs], (cs, GLA_DK)) for bc in bs], axis=0)
    q = q_ref[0].astype(F32) * scale
    k = k_ref[0].astype(F32)
    v = v_ref[0]
    qd = (q * jnp.exp(b)).astype(BF)
    ki = (k * jnp.exp(-b)).astype(BF)
    ke = (k * jnp.exp(btot - b)).astype(BF)
    sc = jnp.where(mk_ref[...] > 0.0, _dot_nt(qd, ki), 0.0).astype(BF)
    o = _dot(sc, v)
    ds = [_dot_tn(v[c * cs:(c + 1) * cs], ke[c * cs:(c + 1) * cs]) for c in range(nc)]
    st = st_ref[...]
    inter = []
    for c in range(nc):
        inter.append(_dot_nt(qd[c * cs:(c + 1) * cs], st.astype(BF)))
        st = st * jnp.exp(bs[c][cs - 1:cs]) + ds[c]
    st_ref[...] = st
    o = o + jnp.concatenate(inter, axis=0)
    o_ref[0] = (_rms(o) * ng_ref[...] * gg_ref[0].astype(F32)).astype(BF)


def _gla_consts(t):
    cs = GLA_CHUNK
    r = jnp.arange(t)[:, None]
    c = jnp.arange(t)[None, :]
    mask = ((r // cs) == (c // cs)) & (c <= r)
    low = mask[:cs, :cs].astype(BF)
    return jnp.concatenate([low, low], axis=1), mask.astype(F32)


def _gla_branch(gla, a2p, ab, ng):
    B, S, _ = gla.shape
    t = min(T_GLA, S)
    H = GLA_HEADS
    col = lambda off: (lambda b, h, s: (b, s, off + h))
    par = lambda b, h, s: (0, h)
    const = lambda b, h, s: (0, 0)
    lc, mk = _gla_consts(t)
    return pl.pallas_call(
        functools.partial(_gla_kernel, t=t),
        grid=(B, H, S // t),
        in_specs=[
            pl.BlockSpec((1, t, LANES), col(0)),
            pl.BlockSpec((1, t, LANES), col(H)),
            pl.BlockSpec((1, t, LANES), col(2 * H)),
            pl.BlockSpec((1, t, LANES), col(3 * H)),
            pl.BlockSpec((1, t, LANES), lambda b, h, s: (b, s, 4 * H)),
            pl.BlockSpec((LANES, GLA_DK), par),
            pl.BlockSpec((1, GLA_DK), par),
            pl.BlockSpec((1, GLA_DV), par),
            pl.BlockSpec((GLA_CHUNK, 2 * GLA_CHUNK), const),
            pl.BlockSpec((t, t), const),
        ],
        out_specs=pl.BlockSpec((1, t, LANES), lambda b, h, s: (b, s, h)),
        out_shape=jax.ShapeDtypeStruct((B, S, GLA_DIM), BF),
        scratch_shapes=[pltpu.VMEM((GLA_DV, GLA_DK), F32)],
        compiler_params=_cparams(("parallel", "parallel", "arbitrary")),
        name="gla_branch",
    )(gla, gla, gla, gla, gla, a2p, ab.reshape(1, GLA_DIM), ng.reshape(1, GLA_DIM), lc, mk)


def _mixer_out_kernel(x_ref, ca_ref, ob_ref, oc_ref, gt_ref, wc_ref, wb_ref, wg_ref, wo_ref,
                      gm_ref, pg_ref, o_ref):
    D = x_ref.shape[-1]
    gt = gt_ref[0]
    m = gt[:, 0:D].astype(F32) * _dot(ca_ref[0], wc_ref[...])
    m = m + gt[:, D:2 * D].astype(F32) * _dot(ob_ref[0], wb_ref[...])
    m = m + gt[:, 2 * D:3 * D].astype(F32) * _dot(oc_ref[0], wg_ref[...])
    y = _dot(m.astype(BF), wo_ref[...])
    o_ref[0] = x_ref[0] + gm_ref[0] * (_rms(y) * pg_ref[...])


def _mixer_out(x, ca, ob, oc, gates, wc, wb, wg, wo, modr, l, post_g):
    B, S, D = x.shape
    tm = min(TM_OUT, S)
    tok = lambda w: pl.BlockSpec((1, tm, w), lambda b, i: (b, i, 0))
    full = lambda a: pl.BlockSpec(a.shape, lambda b, i: (0, 0))
    return pl.pallas_call(
        _mixer_out_kernel,
        grid=(B, S // tm),
        in_specs=[tok(D), tok(CONV_DIM), tok(SB_DIM), tok(GLA_DIM), tok(C_GATE),
                  full(wc), full(wb), full(wg), full(wo),
                  _mod_spec(l, 2, B, D, lambda b, i: b),
                  pl.BlockSpec((1, D), lambda b, i: (0, 0))],
        out_specs=tok(D),
        out_shape=jax.ShapeDtypeStruct((B, S, D), F32),
        compiler_params=_cparams(("parallel", "parallel")),
        name="mixer_out",
    )(x, ca, ob, oc, gates, wc, wb, wg, wo, modr, post_g.reshape(1, D))


def _ffn_kernel(x_ref, sc_ref, sh_ref, pre_ref, wg_ref, wu_ref, wd_ref, gf_ref, post_ref, o_ref,
                h_ref, acc_ref):
    j = pl.program_id(2)

    @pl.when(j == 0)
    def _():
        h_ref[...] = _prenorm(x_ref[0], pre_ref[...], sc_ref[0], sh_ref[0]).astype(BF)
        acc_ref[...] = jnp.zeros_like(acc_ref)

    h = h_ref[...]
    g = _dot(h, wg_ref[...])
    u = _dot(h, wu_ref[...])
    acc_ref[...] += _dot((g * _sigmoid(g) * u).astype(BF), wd_ref[...])

    @pl.when(j == pl.num_programs(2) - 1)
    def _():
        o_ref[0] = x_ref[0] + gf_ref[0] * (_rms(acc_ref[...]) * post_ref[...])


def _ffn_dense(x, modr, l, pre_g, post_g, wg, wu, wd):
    B, S, D = x.shape
    F = wg.shape[1]
    tm = min(TM_FFN, S)
    tf = F // 2 if (F // 2) % LANES == 0 else F
    bof = lambda b, i, j: b
    return pl.pallas_call(
        _ffn_kernel,
        grid=(B, S // tm, F // tf),
        in_specs=[
            pl.BlockSpec((1, tm, D), lambda b, i, j: (b, i, 0)),
            _mod_spec(l, 4, B, D, bof),
            _mod_spec(l, 3, B, D, bof),
            pl.BlockSpec((1, D), lambda b, i, j: (0, 0)),
            pl.BlockSpec((D, tf), lambda b, i, j: (0, j)),
            pl.BlockSpec((D, tf), lambda b, i, j: (0, j)),
            pl.BlockSpec((tf, D), lambda b, i, j: (j, 0)),
            _mod_spec(l, 5, B, D, bof),
            pl.BlockSpec((1, D), lambda b, i, j: (0, 0)),
        ],
        out_specs=pl.BlockSpec((1, tm, D), lambda b, i, j: (b, i, 0)),
        out_shape=jax.ShapeDtypeStruct((B, S, D), F32),
        scratch_shapes=[pltpu.VMEM((tm, D), BF), pltpu.VMEM((tm, D), F32)],
        compiler_params=_cparams(("parallel", "parallel", "arbitrary")),
        name="ffn_dense",
    )(x, modr, modr, pre_g.reshape(1, D), wg, wu, wd, modr, post_g.reshape(1, D))


def _router_kernel(x_ref, sc_ref, sh_ref, pre_ref, rw_ref, h_ref, comb_ref):
    h = _prenorm(x_ref[0], pre_ref[...], sc_ref[0], sh_ref[0])
    h_ref[0] = h.astype(BF)
    hh, hl = _split_bf16(h)
    rh, rl = _split_bf16(rw_ref[...])
    lg = _dot(hh, rh) + _dot(hh, rl) + _dot(hl, rh)
    lane = lax.broadcasted_iota(I32, lg.shape, 1).astype(F32)
    ninf = jnp.float32(-jnp.inf)
    lg = jnp.where(lane < N_EXPERTS, lg, ninf)
    m1 = jnp.max(lg, axis=-1, keepdims=True)
    i1 = jnp.min(jnp.where(lg == m1, lane, float(LANES)), axis=-1, keepdims=True)
    lg2 = jnp.where(lane == i1, ninf, lg)
    m2 = jnp.max(lg2, axis=-1, keepdims=True)
    i2 = jnp.min(jnp.where(lg2 == m2, lane, float(LANES)), axis=-1, keepdims=True)
    e = jnp.exp(m2 - m1)
    p1 = 1.0 / (1.0 + e)
    comb_ref[0] = jnp.where(lane == i1, p1, jnp.where(lane == i2, e * p1, -1.0))


def _router(x, modr, l, pre_g, rw_p):
    B, S, D = x.shape
    tm = min(TM_FFN, S)
    bof = lambda b, i: b
    return pl.pallas_call(
        _router_kernel,
        grid=(B, S // tm),
        in_specs=[
            pl.BlockSpec((1, tm, D), lambda b, i: (b, i, 0)),
            _mod_spec(l, 4, B, D, bof),
            _mod_spec(l, 3, B, D, bof),
            pl.BlockSpec((1, D), lambda b, i: (0, 0)),
            pl.BlockSpec((D, LANES), lambda b, i: (0, 0)),
        ],
        out_specs=[pl.BlockSpec((1, tm, D), lambda b, i: (b, i, 0)),
                   pl.BlockSpec((1, tm, LANES), lambda b, i: (b, i, 0))],
        out_shape=[jax.ShapeDtypeStruct((B, S, D), BF), jax.ShapeDtypeStruct((B, S, LANES), F32)],
        compiler_params=_cparams(("parallel", "parallel")),
        name="moe_router",
    )(x, modr, modr, pre_g.reshape(1, D), rw_p)


def _moe_plan(comb, n_tok):
    E = N_EXPERTS
    nck = n_tok // CK_MOE
    sel = comb[:, :E] >= 0.0
    seli = sel.astype(I32)
    rank_incl = jnp.cumsum(seli, axis=0)
    cnt = rank_incl[-1]
    padded = ((cnt + TM_MOE - 1) // TM_MOE) * TM_MOE
    seg_end = jnp.cumsum(padded)
    seg_start = seg_end - padded
    total = seg_end[-1]
    pos = jnp.where(sel, seg_start[None, :] + rank_incl - seli, -1)
    pos_ck = pos.reshape(nck, CK_MOE, E).transpose(0, 2, 1)
    comb_ck = jnp.where(sel, comb[:, :E], 0.0).reshape(nck, CK_MOE, E).transpose(0, 2, 1)
    cnt_ck = seli.reshape(nck, CK_MOE, E).sum(axis=1)
    cex = jnp.concatenate([jnp.zeros((1, E), I32), jnp.cumsum(cnt_ck, axis=0)], axis=0)

    rows_pad = 2 * n_tok + E * TM_MOE
    jt = jnp.arange(rows_pad // TG_MOE, dtype=I32) * TG_MOE
    te = jnp.minimum(jnp.sum(seg_end[None, :] <= jt[:, None], axis=1), E - 1).astype(I32)
    r0 = jt - seg_start[te]
    r1 = jnp.minimum(r0 + TG_MOE, cnt[te])
    valid = (jt < total) & (r1 > r0)
    cte = cex[1:, :][:, te]
    c_lo = jnp.sum(cte <= r0[None, :], axis=0).astype(I32)
    c_hi = jnp.sum(cte <= (r1 - 1)[None, :], axis=0).astype(I32)
    c_lo = jnp.where(valid, c_lo, 1)
    c_hi = jnp.where(valid, jnp.minimum(c_hi, nck - 1), 0)
    jf = jnp.arange(rows_pad // TM_MOE, dtype=I32) * TM_MOE
    fe = jnp.minimum(jnp.sum(seg_end[None, :] <= jf[:, None], axis=1), E - 1).astype(I32)
    real = jnp.clip(seg_start[fe] + cnt[fe] - jf, 0, TM_MOE)
    hm = TM_MOE // 2
    fvalid = jnp.where(jf < total, (real + hm - 1) // hm, 0).astype(I32)
    w0 = seg_start[None, :] + cex[:-1, :]
    nb = rows_pad // CK_MOE
    b0 = jnp.minimum(w0 // CK_MOE, nb - 1).astype(I32)
    b1 = jnp.minimum(b0 + 1, nb - 1).astype(I32)
    dup = (b1 == b0).astype(I32)
    return dict(pos_ck=pos_ck, comb_ck=comb_ck, te=te, c_lo=c_lo, c_hi=c_hi, fe=fe, fvalid=fvalid,
                b0=b0.reshape(-1), b1=b1.reshape(-1), dup=dup.reshape(-1), rows_pad=rows_pad)


def _gather_kernel(te_ref, clo_ref, chi_ref, h_ref, pos_ref, o_ref, acc_ref):
    j = pl.program_id(0)
    e = te_ref[j]
    rows = j * TG_MOE + lax.broadcasted_iota(I32, (TG_MOE, CK_MOE), 0)
    acc_ref[...] = jnp.zeros_like(acc_ref)

    def body(c, carry):
        p = pos_ref[c, pl.ds(e, 1), :]
        onehot = jnp.where(rows == p, 1.0, 0.0).astype(BF)
        start = pl.multiple_of(c * CK_MOE, CK_MOE)
        acc_ref[...] += _dot(onehot, h_ref[pl.ds(start, CK_MOE), :])
        return carry

    lax.fori_loop(clo_ref[j], chi_ref[j] + 1, body, 0)
    o_ref[...] = acc_ref[...].astype(BF)


def _moe_gather(h2, plan):
    n_tok, D = h2.shape
    rows_pad = plan["rows_pad"]
    grid_spec = pltpu.PrefetchScalarGridSpec(
        num_scalar_prefetch=3,
        grid=(rows_pad // TG_MOE,),
        in_specs=[pl.BlockSpec(memory_space=pltpu.VMEM), pl.BlockSpec(memory_space=pltpu.VMEM)],
        out_specs=pl.BlockSpec((TG_MOE, D), lambda j, *_: (j, 0)),
        scratch_shapes=[pltpu.VMEM((TG_MOE, D), F32)],
    )
    return pl.pallas_call(
        _gather_kernel,
        grid_spec=grid_spec,
        out_shape=jax.ShapeDtypeStruct((rows_pad, D), BF),
        compiler_params=_cparams(("arbitrary",)),
        name="moe_gather",
    )(plan["te"], plan["c_lo"], plan["c_hi"], h2, plan["pos_ck"])


def _moe_ffn_kernel(fe_ref, fv_ref, x_ref, wg_ref, wu_ref, wd_ref, o_ref, acc_ref):
    j = pl.program_id(0)
    f = pl.program_id(1)

    @pl.when(f == 0)
    def _():
        acc_ref[...] = jnp.zeros_like(acc_ref)

    hm = TM_MOE // 2
    for half in range(2):
        @pl.when(fv_ref[j] > half)
        def _(half=half):
            rows = slice(half * hm, (half + 1) * hm)
            x = x_ref[rows, :]
            g = _dot(x, wg_ref[0, 0].astype(BF))
            u = _dot(x, wu_ref[0, 0].astype(BF))
            acc_ref[rows, :] += _dot((g * _sigmoid(g) * u).astype(BF), wd_ref[0, 0].astype(BF))

    @pl.when(f == pl.num_programs(1) - 1)
    def _():
        o_ref[...] = acc_ref[...].astype(BF)


def _moe_ffn(xs, plan, wg, wu, wd, lm):
    rows_pad, D = xs.shape
    F = wg.shape[3]
    tf = TF_MOE if F % TF_MOE == 0 else F
    nf = F // tf
    def fidx(f, fv, j):
        live = jnp.minimum(fv[j], 1)
        return f * live + (nf - 1) * (1 - live)

    grid_spec = pltpu.PrefetchScalarGridSpec(
        num_scalar_prefetch=2,
        grid=(rows_pad // TM_MOE, nf),
        in_specs=[
            pl.BlockSpec((TM_MOE, D), lambda j, f, fe, fv: (j, 0)),
            pl.BlockSpec((1, 1, D, tf), lambda j, f, fe, fv: (lm, fe[j], 0, fidx(f, fv, j))),
            pl.BlockSpec((1, 1, D, tf), lambda j, f, fe, fv: (lm, fe[j], 0, fidx(f, fv, j))),
            pl.BlockSpec((1, 1, tf, D), lambda j, f, fe, fv: (lm, fe[j], fidx(f, fv, j), 0)),
        ],
        out_specs=pl.BlockSpec((TM_MOE, D), lambda j, f, fe, fv: (j, 0)),
        scratch_shapes=[pltpu.VMEM((TM_MOE, D), F32)],
    )
    return pl.pallas_call(
        _moe_ffn_kernel,
        grid_spec=grid_spec,
        out_shape=jax.ShapeDtypeStruct((rows_pad, D), BF),
        compiler_params=_cparams(("arbitrary", "arbitrary")),
        name="moe_ffn",
    )(plan["fe"], plan["fvalid"], xs, wg, wu, wd)


def _combine_kernel(b0_ref, b1_ref, dup_ref, *refs):
    E = N_EXPERTS
    y_refs = refs[:2 * E]
    pos_ref, comb_ref, x_ref, gf_ref, post_ref, o_ref = refs[2 * E:]
    c = pl.program_id(0)
    r = lax.broadcasted_iota(I32, (CK_MOE, CK_MOE), 0)
    acc = jnp.zeros(o_ref.shape, F32)
    for e in range(E):
        idx = c * E + e
        p = pos_ref[0, e:e + 1, :]
        w = comb_ref[0, e:e + 1, :]
        w0 = jnp.where(p == b0_ref[idx] * CK_MOE + r, w, 0.0).astype(BF)
        keep = (1 - dup_ref[idx]).astype(F32)
        w1 = jnp.where(p == b1_ref[idx] * CK_MOE + r, w * keep, 0.0).astype(BF)
        acc = acc + _dot_tn(w0, y_refs[2 * e][...])
        acc = acc + _dot_tn(w1, y_refs[2 * e + 1][...])
    o_ref[...] = x_ref[...] + gf_ref[0] * (_rms(acc) * post_ref[...])


def _moe_combine(x2, ys, plan, modr, l, post_g, B):
    n_tok, D = x2.shape
    nck = n_tok // CK_MOE
    ck_per_b = nck // B
    E = N_EXPERTS
    base = (l * N_ADA + 5) * B
    y_specs = []
    for e in range(E):
        y_specs.append(pl.BlockSpec((CK_MOE, D), lambda c, b0, b1, du, e=e: (b0[c * E + e], 0)))
        y_specs.append(pl.BlockSpec((CK_MOE, D), lambda c, b0, b1, du, e=e: (b1[c * E + e], 0)))
    grid_spec = pltpu.PrefetchScalarGridSpec(
        num_scalar_prefetch=3,
        grid=(nck,),
        in_specs=y_specs + [
            pl.BlockSpec((1, E, CK_MOE), lambda c, *_: (c, 0, 0)),
            pl.BlockSpec((1, E, CK_MOE), lambda c, *_: (c, 0, 0)),
            pl.BlockSpec((CK_MOE, D), lambda c, *_: (c, 0)),
            pl.BlockSpec((1, 1, D), lambda c, *_: (base + c // ck_per_b, 0, 0)),
            pl.BlockSpec((1, D), lambda c, *_: (0, 0)),
        ],
        out_specs=pl.BlockSpec((CK_MOE, D), lambda c, *_: (c, 0)),
    )
    return pl.pallas_call(
        _combine_kernel,
        grid_spec=grid_spec,
        out_shape=jax.ShapeDtypeStruct((n_tok, D), F32),
        compiler_params=_cparams(("arbitrary",)),
        name="moe_combine",
    )(plan["b0"], plan["b1"], plan["dup"], *([ys] * (2 * E)), plan["pos_ck"], plan["comb_ck"],
      x2, modr, post_g.reshape(1, D))


def _moe_layer(x, modr, l, pre_g, post_g, router_w, wg, wu, wd, lm):
    B, S, D = x.shape
    n_tok = B * S
    rw_p = jnp.zeros((D, LANES), F32).at[:, :N_EXPERTS].set(router_w)
    h, comb = _router(x, modr, l, pre_g, rw_p)
    plan = _moe_plan(comb.reshape(n_tok, LANES), n_tok)
    xs = _moe_gather(h.reshape(n_tok, D), plan)
    ys = _moe_ffn(xs, plan, wg, wu, wd, lm)
    out = _moe_combine(x.reshape(n_tok, D), ys, plan, modr, l, post_g, B)
    return out.reshape(B, S, D)


def _pad_w_in(w_in_l):
    split = C_CONV + C_SB + 4 * GLA_DIM + GLA_RANK
    D = w_in_l.shape[0]
    pad = jnp.zeros((D, LANES - GLA_RANK), w_in_l.dtype)
    return jnp.concatenate([w_in_l[:, :split], pad, w_in_l[:, split:]], axis=1).astype(BF)


def kernel(x, c, ada_w, ada_b, mix_pre_g, mix_post_g, ffn_pre_g, ffn_post_g, w_in, gla_a2, gla_a_b, conv_dw, conv_dw_b, conv_ln_g, conv_ln_b, gla_norm_g, w_conv_out, w_sb_out, w_gla_out, w_o, ffn_w_gate, ffn_w_up, ffn_w_down, router_w, moe_w_gate, moe_w_up, moe_w_down):
    depth = w_in.shape[0]
    modr = _ada_mod(c, ada_w, ada_b)
    for l in range(depth):
        glu, sb, gla, gates = _mixer_in(x, modr, l, mix_pre_g[l], _pad_w_in(w_in[l]))
        ca = _conv_branch(glu, conv_dw[l], conv_dw_b[l], conv_ln_g[l], conv_ln_b[l])
        ob = _sb_attention(sb)
        a2p = jnp.zeros((LANES, GLA_DIM), F32).at[:GLA_RANK].set(gla_a2[l]).astype(BF)
        oc = _gla_branch(gla, a2p, gla_a_b[l], gla_norm_g[l])
        x = _mixer_out(x, ca, ob, oc, gates, w_conv_out[l].astype(BF), w_sb_out[l].astype(BF),
                       w_gla_out[l].astype(BF), w_o[l].astype(BF), modr, l, mix_post_g[l])
        j = l // 2
        if l % 2 == 0:
            x = _ffn_dense(x, modr, l, ffn_pre_g[l], ffn_post_g[l], ffn_w_gate[j].astype(BF),
                           ffn_w_up[j].astype(BF), ffn_w_down[j].astype(BF))
        else:
            x = _moe_layer(x, modr, l, ffn_pre_g[l], ffn_post_g[l], router_w[j],
                           moe_w_gate, moe_w_up, moe_w_down, j)
    return x
```
